```python
import math
import jax, jax.numpy as jnp
from jax import lax
import numpy as np

D_MODEL = 2048
BATCH = 8
SEQ = 2048
DEPTH = 4
DEC_BATCH = 4
DEC_SEQ = 2048
PAST_LEN = 128

A_HEADS = 8
A_HEAD_DIM = 64
A_WIDTH = A_HEADS * 2 * A_HEAD_DIM
B_PATTERNS = ((128, 1), (512, 4), (2048, 16))
B_GROUPS = len(B_PATTERNS)
B_HEADS = 4
B_HEAD_DIM = 128
B_WIDTH = B_GROUPS * B_HEADS * B_HEAD_DIM
B_OUT = B_HEADS * B_HEAD_DIM
GATE_WIDTH = 2 * D_MODEL
IN_WIDTH = 3 * A_WIDTH + 3 * B_WIDTH + GATE_WIDTH
IN_SPLITS = (A_WIDTH, 2 * A_WIDTH, 3 * A_WIDTH,
             3 * A_WIDTH + B_WIDTH, 3 * A_WIDTH + 2 * B_WIDTH, 3 * A_WIDTH + 3 * B_WIDTH,
             3 * A_WIDTH + 3 * B_WIDTH + D_MODEL)
D_FF = -(-8 * D_MODEL // (3 * 256)) * 256
ROPE_THETA = 500000.0
ROPE_FRAC = 4
Q_BLOCK = 128
NORM_EPS = 1e-6
MASK_VALUE = -1e30

kernel_name = "hybrid_diff_dilated_encoder"


def rms_norm(x, g):
    xf = x.astype(jnp.float32)
    y = xf * lax.rsqrt(jnp.mean(xf * xf, axis=-1, keepdims=True) + NORM_EPS)
    return (y * g.astype(jnp.float32)).astype(x.dtype)


def rope_tables(seq, rot_dim):
    inv = ROPE_THETA ** (-(jnp.arange(0, rot_dim, 2, dtype=jnp.float32) / rot_dim))
    ang = jnp.arange(seq, dtype=jnp.float32)[:, None] * inv[None, :]
    return jnp.cos(ang), jnp.sin(ang)


def apply_partial_rope(x, cos, sin):
    half = cos.shape[-1]
    rot = 2 * half
    x1 = x[..., :half].astype(jnp.float32)
    x2 = x[..., half:rot].astype(jnp.float32)
    c = cos[None, :, None, :]
    s = sin[None, :, None, :]
    rotated = jnp.concatenate([x1 * c - x2 * s, x2 * c + x1 * s], axis=-1).astype(x.dtype)
    return jnp.concatenate([rotated, x[..., rot:]], axis=-1)


def lambda_init_for(layer_idx):
    return 0.8 - 0.6 * math.exp(-0.3 * layer_idx)


def differential_attention(q, k, v, lam, subln_g, lam_init):
    bn, s_len = q.shape[0], q.shape[1]
    nblk = s_len // Q_BLOCK
    scale = A_HEAD_DIM ** -0.5
    qb = q.reshape(bn, nblk, Q_BLOCK, 2 * A_HEADS, A_HEAD_DIM).transpose(1, 0, 2, 3, 4)

    def one_block(qi):
        s = jnp.einsum('bqhd,bkhd->bhqk', qi, k, preferred_element_type=jnp.float32) * scale
        p = jax.nn.softmax(s, axis=-1).reshape(bn, A_HEADS, 2, Q_BLOCK, s_len)
        a = p[:, :, 0] - lam * p[:, :, 1]
        return jnp.einsum('bhqk,bkhe->bqhe', a.astype(v.dtype), v)

    o = lax.map(one_block, qb)
    o = o.transpose(1, 0, 2, 3, 4).reshape(bn, s_len, A_HEADS, 2 * A_HEAD_DIM)
    o = rms_norm(o, subln_g) * (1.0 - lam_init)
    return o.reshape(bn, s_len, A_WIDTH)


def dilated_attention(q, k, v, window, dilation):
    bn, s_len, nh, dh = q.shape
    half = window // (2 * dilation)
    blk = half
    n_sub = s_len // dilation
    nb = -(-n_sub // blk)
    lp = nb * blk
    bb = bn * dilation

    def to_sub(t):
        t = t.reshape(bn, n_sub, dilation, nh, dh).transpose(0, 2, 1, 3, 4)
        return t.reshape(bb, n_sub, nh, dh)

    qs, ks, vs = to_sub(q), to_sub(k), to_sub(v)
    qs = jnp.pad(qs, ((0, 0), (0, lp - n_sub), (0, 0), (0, 0))).reshape(bb, nb, blk, nh, dh)

    def windows(t):
        t = jnp.pad(t, ((0, 0), (blk, lp - n_sub + blk), (0, 0), (0, 0))).reshape(bb, nb + 2, blk, nh, dh)
        return jnp.concatenate([t[:, :-2], t[:, 1:-1], t[:, 2:]], axis=2)

    kw, vw = windows(ks), windows(vs)
    n_idx = jnp.arange(nb)[:, None, None]
    qi = n_idx * blk + jnp.arange(blk)[None, :, None]
    ki = (n_idx - 1) * blk + jnp.arange(3 * blk)[None, None, :]
    mask = (jnp.abs(ki - qi) <= half) & (ki >= 0) & (ki < n_sub)

    s = jnp.einsum('bnqhd,bnkhd->bnhqk', qs, kw, preferred_element_type=jnp.float32) * (dh ** -0.5)
    s = jnp.where(mask[None, :, None], s, MASK_VALUE)
    lse = jax.nn.logsumexp(s, axis=-1)
    p = jnp.exp(s - lse[..., None])
    o = jnp.einsum('bnhqk,bnkhd->bnqhd', p.astype(v.dtype), vw)

    o = o.reshape(bb, lp, nh, dh)[:, :n_sub]
    o = o.reshape(bn, dilation, n_sub, nh, dh).transpose(0, 2, 1, 3, 4).reshape(bn, s_len, nh, dh)
    lse = lse.transpose(0, 1, 3, 2).reshape(bb, lp, nh)[:, :n_sub]
    lse = lse.reshape(bn, dilation, n_sub, nh).transpose(0, 2, 1, 3).reshape(bn, s_len, nh)
    return o, lse


def token_mixer(xn, w_in, gate_bias, lam_p, subln_g, w_pa, w_pb, w_out, lam_init,
                cos_a, sin_a, cos_b, sin_b):
    bn, s_len, _ = xn.shape
    z = jnp.einsum('bsd,de->bse', xn, w_in)
    qa, ka, va, qb, kb, vb, ga, gb = jnp.split(z, IN_SPLITS, axis=-1)

    qa = apply_partial_rope(qa.reshape(bn, s_len, 2 * A_HEADS, A_HEAD_DIM), cos_a, sin_a)
    ka = apply_partial_rope(ka.reshape(bn, s_len, 2 * A_HEADS, A_HEAD_DIM), cos_a, sin_a)
    va = va.reshape(bn, s_len, A_HEADS, 2 * A_HEAD_DIM)
    lf = lam_p.astype(jnp.float32)
    lam = jnp.exp(jnp.sum(lf[0] * lf[1])) - jnp.exp(jnp.sum(lf[2] * lf[3])) + lam_init
    out_a = differential_attention(qa, ka, va, lam, subln_g, lam_init)

    nbh = B_GROUPS * B_HEADS
    qb = apply_partial_rope(qb.reshape(bn, s_len, nbh, B_HEAD_DIM), cos_b, sin_b)
    kb = apply_partial_rope(kb.reshape(bn, s_len, nbh, B_HEAD_DIM), cos_b, sin_b)
    qb = qb.reshape(bn, s_len, B_GROUPS, B_HEADS, B_HEAD_DIM)
    kb = kb.reshape(bn, s_len, B_GROUPS, B_HEADS, B_HEAD_DIM)
    vb = vb.reshape(bn, s_len, B_GROUPS, B_HEADS, B_HEAD_DIM)
    outs, lses = [], []
    for g, (window, dilation) in enumerate(B_PATTERNS):
        o, lse = dilated_attention(qb[:, :, g], kb[:, :, g], vb[:, :, g], window, dilation)
        outs.append(o)
        lses.append(lse)
    wts = jax.nn.softmax(jnp.stack(lses, axis=0), axis=0)
    out_b = jnp.sum(wts[..., None] * jnp.stack(outs, axis=0).astype(jnp.float32), axis=0)
    out_b = out_b.astype(xn.dtype).reshape(bn, s_len, B_OUT)

    ya = jnp.einsum('bse,ed->bsd', out_a, w_pa)
    yb = jnp.einsum('bse,ed->bsd', out_b, w_pb)
    gate_a = jax.nn.sigmoid(ga + gate_bias[0])
    gate_b = jax.nn.sigmoid(gb + gate_bias[1])
    return jnp.einsum('bsd,de->bse', gate_a * ya + gate_b * yb, w_out)


def swiglu(xn, w_ffn_in, w_ffn_out):
    h = jnp.einsum('bsd,df->bsf', xn, w_ffn_in)
    g, u = jnp.split(h, 2, axis=-1)
    return jnp.einsum('bsf,fd->bsd', jax.nn.silu(g) * u, w_ffn_out)


def encoder_trunk(x, norm_mix, norm_ffn, w_in, gate_bias, diff_lambda, diff_subln,
                  w_proj_a, w_proj_b, w_out, w_ffn_in, w_ffn_out, norm_final):
    s_len = x.shape[1]
    cos_a, sin_a = rope_tables(s_len, A_HEAD_DIM // ROPE_FRAC)
    cos_b, sin_b = rope_tables(s_len, B_HEAD_DIM // ROPE_FRAC)
    for l in range(DEPTH):
        xn = rms_norm(x, norm_mix[l])
        x = x + token_mixer(xn, w_in[l], gate_bias[l], diff_lambda[l], diff_subln[l],
                            w_proj_a[l], w_proj_b[l], w_out[l], lambda_init_for(l),
                            cos_a, sin_a, cos_b, sin_b)
        hn = rms_norm(x, norm_ffn[l])
        x = x + swiglu(hn, w_ffn_in[l], w_ffn_out[l])
    return rms_norm(x, norm_final)


def setup_inputs(seed: int = 0) -> dict:
    key = jax.random.key(seed)
    ks = jax.random.split(key, 14)

    def nrm(k, shape, scale):
        return jax.random.normal(k, shape, jnp.float32) * scale

    return {
        'x_prompt': nrm(ks[0], (BATCH, SEQ, D_MODEL), 1.0),
        'x_sample': nrm(ks[1], (DEC_BATCH, DEC_SEQ, D_MODEL), 1.0),
        'norm_mix': 1.0 + nrm(ks[2], (DEPTH, D_MODEL), 0.02),
        'norm_ffn': 1.0 + nrm(ks[3], (DEPTH, D_MODEL), 0.02),
        'w_in': nrm(ks[4], (DEPTH, D_MODEL, IN_WIDTH), D_MODEL ** -0.5),
        'gate_bias': nrm(ks[5], (DEPTH, 2, D_MODEL), 0.1),
        'diff_lambda': nrm(ks[6], (DEPTH, 4, A_HEAD_DIM), 0.1),
        'diff_subln': 1.0 + nrm(ks[7], (DEPTH, 2 * A_HEAD_DIM), 0.02),
        'w_proj_a': nrm(ks[8], (DEPTH, A_WIDTH, D_MODEL), A_WIDTH ** -0.5),
        'w_proj_b': nrm(ks[9], (DEPTH, B_OUT, D_MODEL), B_OUT ** -0.5),
        'w_out': nrm(ks[10], (DEPTH, D_MODEL, D_MODEL), D_MODEL ** -0.5),
        'w_ffn_in': nrm(ks[11], (DEPTH, D_MODEL, 2 * D_FF), D_MODEL ** -0.5),
        'w_ffn_out': nrm(ks[12], (DEPTH, D_FF, D_MODEL), D_FF ** -0.5),
        'norm_final': 1.0 + nrm(ks[13], (D_MODEL,), 0.02),
    }


def reference(x_prompt, x_sample, norm_mix, norm_ffn, w_in, gate_bias, diff_lambda, diff_subln,
              w_proj_a, w_proj_b, w_out, w_ffn_in, w_ffn_out, norm_final):
    y_prompt = encoder_trunk(x_prompt, norm_mix, norm_ffn, w_in, gate_bias, diff_lambda, diff_subln,
                             w_proj_a, w_proj_b, w_out, w_ffn_in, w_ffn_out, norm_final)
    y_sample = encoder_trunk(x_sample, norm_mix, norm_ffn, w_in, gate_bias, diff_lambda, diff_subln,
                             w_proj_a, w_proj_b, w_out, w_ffn_in, w_ffn_out, norm_final)
    return (y_prompt, y_sample)
```

```python
import functools
import math

import jax
import jax.numpy as jnp
from jax import lax
from jax.experimental import pallas as pl
from jax.experimental.pallas import tpu as pltpu

D_MODEL = 2048
DEPTH = 4
A_HEADS = 8
A_HEAD_DIM = 64
A_WIDTH = A_HEADS * 2 * A_HEAD_DIM
B_PATTERNS = ((128, 1), (512, 4), (2048, 16))
B_GROUPS = len(B_PATTERNS)
B_HEADS = 4
B_HEAD_DIM = 128
B_WIDTH = B_GROUPS * B_HEADS * B_HEAD_DIM
B_OUT = B_HEADS * B_HEAD_DIM
QKV_WIDTH = 3 * A_WIDTH + 3 * B_WIDTH
GATE_WIDTH = 2 * D_MODEL
D_FF = 5632
ROPE_THETA = 500000.0
ROPE_FRAC = 4
NORM_EPS = 1e-6
MASK_VALUE = -1e30

LANES = 128
VMEM_LIMIT = 60 * 1024 * 1024

PROJ_TN = 512
DIFF_TQ = 256
DIL_TQ = 128
DIL_HALF = 64
DIL_WIN = 256
MIX_TM = 512
FFN_TM = 512
FFN_TF = 512
NORM_TM = 512

BF16 = jnp.bfloat16
F32 = jnp.float32


def _params(*sem):
    return pltpu.CompilerParams(dimension_semantics=sem, vmem_limit_bytes=VMEM_LIMIT)


def _rms(x, g):
    ms = jnp.mean(x * x, axis=-1, keepdims=True)
    return x * lax.rsqrt(ms + NORM_EPS) * g


def _rmsnorm_kernel(x_ref, g_ref, o_ref):
    o_ref[...] = _rms(x_ref[...], g_ref[...]).astype(o_ref.dtype)


def _rmsnorm(x, g_all, layer):
    t = x.shape[0]
    return pl.pallas_call(
        _rmsnorm_kernel,
        grid=(t // NORM_TM,),
        in_specs=[pl.BlockSpec((NORM_TM, D_MODEL), lambda i: (i, 0)),
                  pl.BlockSpec((None, 1, D_MODEL), lambda i: (layer, 0, 0))],
        out_specs=pl.BlockSpec((NORM_TM, D_MODEL), lambda i: (i, 0)),
        out_shape=jax.ShapeDtypeStruct((t, D_MODEL), BF16),
        compiler_params=_params("parallel"),
        name="rmsnorm_in",
    )(x, g_all)


def _rope_chunk(x, c, s, head_dim):
    half = head_dim // ROPE_FRAC // 2
    lane = lax.broadcasted_iota(jnp.int32, x.shape, 1) % head_dim
    up = pltpu.roll(x, LANES - half, 1)
    down = pltpu.roll(x, half, 1)
    return x * c + jnp.where(lane < half, up, down) * s


def _qkv_kernel(x_ref, w_ref, ca_ref, sa_ref, cb_ref, sb_ref, o_ref, *acc_refs, seq):
    j = pl.program_id(1)
    acc = jnp.dot(x_ref[...], w_ref[...], preferred_element_type=F32)
    lane_chunks = [slice(c * LANES, (c + 1) * LANES) for c in range(len(acc_refs))]
    for ref, sl in zip(acc_refs, lane_chunks):
        ref[...] = acc[:, sl]
    n_a = 2 * A_WIDTH // PROJ_TN
    n_av = 3 * A_WIDTH // PROJ_TN
    n_b = B_WIDTH // PROJ_TN

    @pl.when(j < n_a)
    def _():
        scale = jnp.where(j < n_a // 2, A_HEAD_DIM ** -0.5, 1.0).astype(F32)
        for ref in acc_refs:
            ref[...] = _rope_chunk(ref[...], ca_ref[...], sa_ref[...], A_HEAD_DIM) * scale

    @pl.when((j >= n_av) & (j < n_av + 2 * n_b))
    def _():
        for ref in acc_refs:
            ref[...] = _rope_chunk(ref[...], cb_ref[...], sb_ref[...], B_HEAD_DIM)

    group = jnp.where(j >= n_av, (j - n_av) % n_b, 0)
    for gi, (_, d) in enumerate(B_PATTERNS):
        @pl.when(group == gi)
        def _(d=d):
            n_sub = seq // d
            for ref, sl in zip(acc_refs, lane_chunks):
                for p in range(d):
                    rows = pl.ds(p, n_sub, stride=d) if d > 1 else slice(None)
                    o_ref[p * n_sub:(p + 1) * n_sub, sl] = ref[rows, :].astype(o_ref.dtype)


def _qkv_proj(xn, w_in, layer, tables, seq):
    t = xn.shape[0]
    tab_spec = pl.BlockSpec((seq, LANES), lambda i, j: (0, 0))
    return pl.pallas_call(
        functools.partial(_qkv_kernel, seq=seq),
        grid=(t // seq, QKV_WIDTH // PROJ_TN),
        in_specs=[pl.BlockSpec((seq, D_MODEL), lambda i, j: (i, 0)),
                  pl.BlockSpec((None, D_MODEL, PROJ_TN), lambda i, j: (layer, 0, j)),
                  tab_spec, tab_spec, tab_spec, tab_spec],
        out_specs=pl.BlockSpec((seq, PROJ_TN), lambda i, j: (i, j)),
        out_shape=jax.ShapeDtypeStruct((t, QKV_WIDTH), BF16),
        scratch_shapes=[pltpu.VMEM((seq, LANES), F32)] * (PROJ_TN // LANES),
        compiler_params=_params("parallel", "arbitrary"),
        name="qkv_proj",
    )(xn, w_in, *tables)


def _gate_kernel(x_ref, w_ref, b_ref, o_ref):
    z = jnp.dot(x_ref[...], w_ref[...], preferred_element_type=F32) + b_ref[...]
    o_ref[...] = 1.0 / (1.0 + jnp.exp(-z))


def _gate_proj(xn, w_in, bias, layer, seq):
    t = xn.shape[0]
    col0 = QKV_WIDTH // PROJ_TN
    return pl.pallas_call(
        _gate_kernel,
        grid=(t // seq, GATE_WIDTH // PROJ_TN),
        in_specs=[pl.BlockSpec((seq, D_MODEL), lambda i, j: (i, 0)),
                  pl.BlockSpec((None, D_MODEL, PROJ_TN), lambda i, j: (layer, 0, col0 + j)),
                  pl.BlockSpec((None, 1, PROJ_TN), lambda i, j: (layer, 0, j))],
        out_specs=pl.BlockSpec((seq, PROJ_TN), lambda i, j: (i, j)),
        out_shape=jax.ShapeDtypeStruct((t, GATE_WIDTH), F32),
        compiler_params=_params("parallel", "arbitrary"),
        name="gate_proj",
    )(xn, w_in, bias)


def _nt_dot(a, b):
    return lax.dot_general(a, b, (((1,), (1,)), ((), ())), preferred_element_type=F32)


def _diff_kernel(q_ref, k_ref, v_ref, lam_ref, g_ref, o_ref, *, seq, lam_init):
    lf = lam_ref[...]
    lam = (jnp.exp(jnp.sum(lf[0:1] * lf[1:2], axis=-1, keepdims=True))
           - jnp.exp(jnp.sum(lf[2:3] * lf[3:4], axis=-1, keepdims=True)) + lam_init)
    k = k_ref[...]
    v = v_ref[...]
    g = g_ref[...] * (1.0 - lam_init)
    lane = lax.broadcasted_iota(jnp.int32, (DIFF_TQ, LANES), 1)
    first = lane < A_HEAD_DIM

    def step(t, carry):
        rows = pl.ds(pl.multiple_of(t * DIFF_TQ, DIFF_TQ), DIFF_TQ)
        q = q_ref[rows, :]
        zero = jnp.zeros_like(q)
        s1 = _nt_dot(jnp.where(first, q, zero), k)
        s2 = _nt_dot(jnp.where(first, zero, q), k)
        e1 = jnp.exp(s1 - jnp.max(s1, axis=-1, keepdims=True))
        e2 = jnp.exp(s2 - jnp.max(s2, axis=-1, keepdims=True))
        r1 = 1.0 / jnp.sum(e1, axis=-1, keepdims=True)
        r2 = lam / jnp.sum(e2, axis=-1, keepdims=True)
        a = (e1 * r1 - e2 * r2).astype(v.dtype)
        o = jnp.dot(a, v, preferred_element_type=F32)
        o_ref[rows, :] = _rms(o, g).astype(o_ref.dtype)
        return carry

    lax.fori_loop(0, seq // DIFF_TQ, step, 0)


def _diff_attention(z, lam_all, subln_all, layer, seq):
    t = z.shape[0]
    kb = A_WIDTH // LANES
    blk = lambda off: pl.BlockSpec((seq, LANES), lambda b, h: (b, off + h))
    return pl.pallas_call(
        functools.partial(_diff_kernel, seq=seq, lam_init=0.8 - 0.6 * math.exp(-0.3 * layer)),
        grid=(t // seq, A_HEADS),
        in_specs=[blk(0), blk(kb), blk(2 * kb),
                  pl.BlockSpec((None, 4, A_HEAD_DIM), lambda b, h: (layer, 0, 0)),
                  pl.BlockSpec((None, 1, 2 * A_HEAD_DIM), lambda b, h: (layer, 0, 0))],
        out_specs=pl.BlockSpec((seq, LANES), lambda b, h: (b, h)),
        out_shape=jax.ShapeDtypeStruct((t, A_WIDTH), BF16),
        compiler_params=_params("parallel", "parallel"),
        name="diff_attention",
    )(z, z, z, lam_all, subln_all)


def _dil_kernel(*refs, seq):
    q_refs, k_refs, v_refs = refs[0:3], refs[3:6], refs[6:9]
    o_ref = refs[9]
    o_scr, l_scr = refs[10:13], refs[13:16]
    scale = B_HEAD_DIM ** -0.5
    row = lax.broadcasted_iota(jnp.int32, (DIL_TQ, DIL_WIN), 0)
    col = lax.broadcasted_iota(jnp.int32, (DIL_TQ, DIL_WIN), 1)
    rel = col - row

    for gi, (window, d) in enumerate(B_PATTERNS):
        assert window // (2 * d) == DIL_HALF
        n_sub = seq // d
        nblk = n_sub // DIL_TQ

        def step(t, carry, gi=gi, d=d, n_sub=n_sub, nblk=nblk):
            p = t // nblk
            n = t % nblk
            q0 = pl.multiple_of(t * DIL_TQ, DIL_TQ)
            ws = pl.multiple_of(jnp.clip(q0 - DIL_HALF, 0, seq - DIL_WIN), DIL_HALF)
            lo = p * n_sub - ws
            q = q_refs[gi][pl.ds(q0, DIL_TQ), :]
            k = k_refs[gi][pl.ds(ws, DIL_WIN), :]
            v = v_refs[gi][pl.ds(ws, DIL_WIN), :]
            s = _nt_dot(q, k) * scale
            ok = (jnp.abs(rel + (ws - q0)) <= DIL_HALF) & (col >= lo) & (col < lo + n_sub)
            s = jnp.where(ok, s, MASK_VALUE)
            m = jnp.max(s, axis=-1, keepdims=True)
            e = jnp.exp(s - m)
            l = jnp.sum(e, axis=-1, keepdims=True)
            o = jnp.dot(e.astype(v.dtype), v, preferred_element_type=F32) / l
            lse = jnp.broadcast_to(m + jnp.log(l), (DIL_TQ, LANES))
            dst = pl.ds(n * (DIL_TQ * d) + p, DIL_TQ, stride=d) if d > 1 else pl.ds(q0, DIL_TQ)
            o_scr[gi][dst, :] = o
            l_scr[gi][dst, :] = lse
            return carry

        lax.fori_loop(0, seq // DIL_TQ, step, 0)

    l0, l1, l2 = l_scr[0][...], l_scr[1][...], l_scr[2][...]
    m = jnp.maximum(jnp.maximum(l0, l1), l2)
    e0, e1, e2 = jnp.exp(l0 - m), jnp.exp(l1 - m), jnp.exp(l2 - m)
    num = e0 * o_scr[0][...] + e1 * o_scr[1][...] + e2 * o_scr[2][...]
    o_ref[...] = (num / (e0 + e1 + e2)).astype(o_ref.dtype)


def _dil_attention(z, seq):
    t = z.shape[0]
    base = 3 * A_WIDTH // LANES
    per = B_WIDTH // LANES

    def blk(which, g):
        off = base + which * per + g * B_HEADS
        return pl.BlockSpec((seq, LANES), lambda b, h: (b, off + h))

    in_specs = [blk(w, g) for w in range(3) for g in range(B_GROUPS)]
    return pl.pallas_call(
        functools.partial(_dil_kernel, seq=seq),
        grid=(t // seq, B_HEADS),
        in_specs=in_specs,
        out_specs=pl.BlockSpec((seq, LANES), lambda b, h: (b, h)),
        out_shape=jax.ShapeDtypeStruct((t, B_OUT), BF16),
        scratch_shapes=[pltpu.VMEM((seq, LANES), F32)] * (2 * B_GROUPS),
        compiler_params=_params("parallel", "parallel"),
        name="dilated_attention",
    )(*([z] * 9))


def _mix_kernel(oa_ref, ob_ref, ga_ref, gb_ref, x_ref, wpa_ref, wpb_ref, wo_ref, g_ref,
                xo_ref, hn_ref):
    ya = jnp.dot(oa_ref[...], wpa_ref[...], preferred_element_type=F32)
    yb = jnp.dot(ob_ref[...], wpb_ref[...], preferred_element_type=F32)
    merged = (ga_ref[...] * ya + gb_ref[...] * yb).astype(BF16)
    xo = x_ref[...] + jnp.dot(merged, wo_ref[...], preferred_element_type=F32)
    xo_ref[...] = xo
    hn_ref[...] = _rms(xo, g_ref[...]).astype(hn_ref.dtype)


def _mix_out(oa, ob, gates, x, w_pa, w_pb, w_out, norm_ffn, layer):
    t = x.shape[0]
    row = lambda width, col=0: pl.BlockSpec((MIX_TM, width), lambda i: (i, col))
    whole = lambda k, n: pl.BlockSpec((None, k, n), lambda i: (layer, 0, 0),
                                      pipeline_mode=pl.Buffered(1))
    return pl.pallas_call(
        _mix_kernel,
        grid=(t // MIX_TM,),
        in_specs=[row(A_WIDTH), row(B_OUT), row(D_MODEL, 0), row(D_MODEL, 1), row(D_MODEL),
                  whole(A_WIDTH, D_MODEL), whole(B_OUT, D_MODEL), whole(D_MODEL, D_MODEL),
                  pl.BlockSpec((None, 1, D_MODEL), lambda i: (layer, 0, 0))],
        out_specs=[row(D_MODEL), row(D_MODEL)],
        out_shape=[jax.ShapeDtypeStruct((t, D_MODEL), F32),
                   jax.ShapeDtypeStruct((t, D_MODEL), BF16)],
        compiler_params=_params("parallel"),
        name="mix_out",
    )(oa, ob, gates, gates, x, w_pa, w_pb, w_out, norm_ffn)


def _ffn_kernel(hn_ref, wg_ref, wu_ref, w2_ref, x_ref, g_ref, *rest, emit_x):
    outs, acc_ref = rest[:-1], rest[-1]
    k = pl.program_id(1)
    hn = hn_ref[...]
    hg = jnp.dot(hn, wg_ref[...], preferred_element_type=F32)
    hu = jnp.dot(hn, wu_ref[...], preferred_element_type=F32)
    act = (hg / (1.0 + jnp.exp(-hg)) * hu).astype(BF16)
    part = jnp.dot(act, w2_ref[...], preferred_element_type=F32)

    @pl.when(k == 0)
    def _():
        acc_ref[...] = x_ref[...] + part

    @pl.when(k > 0)
    def _():
        acc_ref[...] += part

    @pl.when(k == pl.num_programs(1) - 1)
    def _():
        xo = acc_ref[...]
        if emit_x:
            outs[0][...] = xo
        outs[-1][...] = _rms(xo, g_ref[...]).astype(outs[-1].dtype)


def _ffn(hn, w_ffn_in, w_ffn_out, x, norm_all, layer, norm_layer, emit_x):
    t = x.shape[0]
    nk = D_FF // FFN_TF
    row = pl.BlockSpec((FFN_TM, D_MODEL), lambda i, k: (i, 0))
    if norm_all.ndim == 3:
        g_spec = pl.BlockSpec((None, 1, D_MODEL), lambda i, k: (norm_layer, 0, 0))
    else:
        g_spec = pl.BlockSpec((1, D_MODEL), lambda i, k: (0, 0))
    out_specs = [row]
    out_shape = [jax.ShapeDtypeStruct((t, D_MODEL), BF16 if emit_x else F32)]
    if emit_x:
        out_specs = [row, row]
        out_shape = [jax.ShapeDtypeStruct((t, D_MODEL), F32)] + out_shape
    return pl.pallas_call(
        functools.partial(_ffn_kernel, emit_x=emit_x),
        grid=(t // FFN_TM, nk),
        in_specs=[row,
                  pl.BlockSpec((None, D_MODEL, FFN_TF), lambda i, k: (layer, 0, k)),
                  pl.BlockSpec((None, D_MODEL, FFN_TF), lambda i, k: (layer, 0, nk + k)),
                  pl.BlockSpec((None, FFN_TF, D_MODEL), lambda i, k: (layer, k, 0)),
                  row, g_spec],
        out_specs=out_specs,
        out_shape=out_shape,
        scratch_shapes=[pltpu.VMEM((FFN_TM, D_MODEL), F32)],
        compiler_params=_params("parallel", "arbitrary"),
        name="ffn",
    )(hn, w_ffn_in, w_ffn_in, w_ffn_out, x, norm_all)


def _rope_lane_tables(seq, head_dim):
    rot = head_dim // ROPE_FRAC
    inv = ROPE_THETA ** (-(jnp.arange(0, rot, 2, dtype=F32) / rot))
    ang = jnp.arange(seq, dtype=F32)[:, None] * inv[None, :]
    cos, sin = jnp.cos(ang), jnp.sin(ang)
    rest = head_dim - rot
    c = jnp.concatenate([cos, cos, jnp.ones((seq, rest), F32)], axis=-1)
    s = jnp.concatenate([-sin, sin, jnp.zeros((seq, rest), F32)], axis=-1)
    reps = LANES // head_dim
    return jnp.tile(c, (1, reps)), jnp.tile(s, (1, reps))


def _trunk(x, seq, norm_mix, norm_ffn, w_in, gate_bias, diff_lambda, diff_subln,
           w_pa, w_pb, w_out, w_ffn_in, w_ffn_out, norm_final):
    tables = _rope_lane_tables(seq, A_HEAD_DIM) + _rope_lane_tables(seq, B_HEAD_DIM)
    xn = _rmsnorm(x, norm_mix, 0)
    for layer in range(DEPTH):
        z = _qkv_proj(xn, w_in, layer, tables, seq)
        gates = _gate_proj(xn, w_in, gate_bias, layer, seq)
        oa = _diff_attention(z, diff_lambda, diff_subln, layer, seq)
        ob = _dil_attention(z, seq)
        x, hn = _mix_out(oa, ob, gates, x, w_pa, w_pb, w_out, norm_ffn, layer)
        if layer + 1 < DEPTH:
            x, xn = _ffn(hn, w_ffn_in, w_ffn_out, x, norm_mix, layer, layer + 1, True)
        else:
            (y,) = _ffn(hn, w_ffn_in, w_ffn_out, x, norm_final, layer, 0, False)
    return y


def kernel(x_prompt, x_sample, norm_mix, norm_ffn, w_in, gate_bias, diff_lambda, diff_subln,
           w_proj_a, w_proj_b, w_out, w_ffn_in, w_ffn_out, norm_final):
    bp, seq, d = x_prompt.shape
    bs = x_sample.shape[0]
    assert x_sample.shape[1:] == (seq, d) and d == D_MODEL
    assert seq % (B_PATTERNS[-1][1] * DIL_TQ) == 0 and seq >= DIL_WIN
    x = jnp.concatenate([x_prompt.reshape(bp * seq, d), x_sample.reshape(bs * seq, d)], axis=0)
    y = _trunk(
        x, seq,
        norm_mix.reshape(DEPTH, 1, D_MODEL), norm_ffn.reshape(DEPTH, 1, D_MODEL),
        w_in.astype(BF16), gate_bias.reshape(DEPTH, 1, GATE_WIDTH),
        diff_lambda, diff_subln.reshape(DEPTH, 1, 2 * A_HEAD_DIM),
        w_proj_a.astype(BF16), w_proj_b.astype(BF16), w_out.astype(BF16),
        w_ffn_in.astype(BF16), w_ffn_out.astype(BF16), norm_final.reshape(1, D_MODEL))
    return (y[:bp * seq].reshape(bp, seq, d), y[bp * seq:].reshape(bs, seq, d))
```

```python
import functools
import math

import jax
import jax.numpy as jnp
from jax import lax
from jax.experimental import pallas as pl
from jax.experimental.pallas import tpu as pltpu

D_MODEL = 2048
DEPTH = 4
A_HEADS = 8
A_HEAD_DIM = 64
A_WIDTH = A_HEADS * 2 * A_HEAD_DIM
B_PATTERNS = ((128, 1), (512, 4), (2048, 16))
B_GROUPS = len(B_PATTERNS)
B_HEADS = 4
B_HEAD_DIM = 128
B_WIDTH = B_GROUPS * B_HEADS * B_HEAD_DIM
B_OUT = B_HEADS * B_HEAD_DIM
QKV_WIDTH = 3 * A_WIDTH + 3 * B_WIDTH
GATE_WIDTH = 2 * D_MODEL
D_FF = 5632
ROPE_THETA = 500000.0
ROPE_FRAC = 4
NORM_EPS = 1e-6
MASK_VALUE = -1e30

LANES = 128
BF16_ROWS = 16
VMEM_LIMIT = 60 * 1024 * 1024

PROJ_TN = 512
DIFF_TQ = 256
DIL_UNROLL = 4
DIL_TQ = 128
DIL_HALF = 64
DIL_WIN = 256
MIX_TM = 512
FFN_TM = 512
FFN_TF = 512
NORM_TM = 512

BF16 = jnp.bfloat16
F32 = jnp.float32


def _params(*sem):
    return pltpu.CompilerParams(dimension_semantics=sem, vmem_limit_bytes=VMEM_LIMIT)


def _rms(x, g):
    ms = jnp.mean(x * x, axis=-1, keepdims=True)
    return x * lax.rsqrt(ms + NORM_EPS) * g


def _rmsnorm_kernel(x_ref, g_ref, o_ref):
    o_ref[...] = _rms(x_ref[...], g_ref[...]).astype(o_ref.dtype)


def _rmsnorm(x, g_all, layer):
    t = x.shape[0]
    return pl.pallas_call(
        _rmsnorm_kernel,
        grid=(t // NORM_TM,),
        in_specs=[pl.BlockSpec((NORM_TM, D_MODEL), lambda i: (i, 0)),
                  pl.BlockSpec((None, 1, D_MODEL), lambda i: (layer, 0, 0))],
        out_specs=pl.BlockSpec((NORM_TM, D_MODEL), lambda i: (i, 0)),
        out_shape=jax.ShapeDtypeStruct((t, D_MODEL), BF16),
        compiler_params=_params("parallel"),
        name="rmsnorm_in",
    )(x, g_all)


def _rope_chunk(x, c, s, head_dim):
    half = head_dim // ROPE_FRAC // 2
    lane = lax.broadcasted_iota(jnp.int32, x.shape, 1) % head_dim
    up = pltpu.roll(x, LANES - half, 1)
    down = pltpu.roll(x, half, 1)
    return x * c + jnp.where(lane < half, up, down) * s


def _qkv_kernel(x_ref, w_ref, ca_ref, sa_ref, cb_ref, sb_ref, o_ref, *acc_refs, seq):
    j = pl.program_id(1)
    acc = jnp.dot(x_ref[...], w_ref[...], preferred_element_type=F32)
    lane_chunks = [slice(c * LANES, (c + 1) * LANES) for c in range(len(acc_refs))]
    n_a = 2 * A_WIDTH // PROJ_TN
    n_av = 3 * A_WIDTH // PROJ_TN
    n_b = B_WIDTH // PROJ_TN
    assert n_b == B_GROUPS

    def finish(rope, d):
        n_sub = seq // d
        for ref, sl in zip(acc_refs, lane_chunks):
            val = acc[:, sl]
            if rope == "a":
                scale = jnp.where(j < n_a // 2, A_HEAD_DIM ** -0.5, 1.0).astype(F32)
                val = _rope_chunk(val, ca_ref[...], sa_ref[...], A_HEAD_DIM) * scale
            elif rope == "b":
                val = _rope_chunk(val, cb_ref[...], sb_ref[...], B_HEAD_DIM)
            if d == 1:
                o_ref[:, sl] = val.astype(o_ref.dtype)
            else:
                ref[...] = val
                for p in range(d):
                    o_ref[p * n_sub:(p + 1) * n_sub, sl] = (
                        ref[pl.ds(p, n_sub, stride=d), :].astype(o_ref.dtype))

    pl.when(j < n_a)(lambda: finish("a", 1))
    pl.when((j >= n_a) & (j < n_av))(lambda: finish(None, 1))
    group = (j - n_av) % n_b
    for gi, (_, d) in enumerate(B_PATTERNS):
        is_group = (j >= n_av) & (group == gi)
        pl.when(is_group & (j < n_av + 2 * n_b))(functools.partial(finish, "b", d))
        pl.when(is_group & (j >= n_av + 2 * n_b))(functools.partial(finish, None, d))


def _qkv_proj(xn, w_in, layer, tables, seq):
    t = xn.shape[0]
    tab_spec = pl.BlockSpec((seq, LANES), lambda i, j: (0, 0))
    return pl.pallas_call(
        functools.partial(_qkv_kernel, seq=seq),
        grid=(t // seq, QKV_WIDTH // PROJ_TN),
        in_specs=[pl.BlockSpec((seq, D_MODEL), lambda i, j: (i, 0)),
                  pl.BlockSpec((None, D_MODEL, PROJ_TN), lambda i, j: (layer, 0, j)),
                  tab_spec, tab_spec, tab_spec, tab_spec],
        out_specs=pl.BlockSpec((seq, PROJ_TN), lambda i, j: (i, j)),
        out_shape=jax.ShapeDtypeStruct((t, QKV_WIDTH), BF16),
        scratch_shapes=[pltpu.VMEM((seq, LANES), F32)] * (PROJ_TN // LANES),
        compiler_params=_params("parallel", "arbitrary"),
        name="qkv_proj",
    )(xn, w_in, *tables)


def _gate_kernel(x_ref, w_ref, b_ref, o_ref):
    z = jnp.dot(x_ref[...], w_ref[...], preferred_element_type=F32) + b_ref[...]
    o_ref[...] = 1.0 / (1.0 + jnp.exp(-z))


def _gate_proj(xn, w_in, bias, layer, seq):
    t = xn.shape[0]
    col0 = QKV_WIDTH // PROJ_TN
    return pl.pallas_call(
        _gate_kernel,
        grid=(t // seq, GATE_WIDTH // PROJ_TN),
        in_specs=[pl.BlockSpec((seq, D_MODEL), lambda i, j: (i, 0)),
                  pl.BlockSpec((None, D_MODEL, PROJ_TN), lambda i, j: (layer, 0, col0 + j)),
                  pl.BlockSpec((None, 1, PROJ_TN), lambda i, j: (layer, 0, j))],
        out_specs=pl.BlockSpec((seq, PROJ_TN), lambda i, j: (i, j)),
        out_shape=jax.ShapeDtypeStruct((t, GATE_WIDTH), F32),
        compiler_params=_params("parallel", "arbitrary"),
        name="gate_proj",
    )(xn, w_in, bias)


def _nt_dot(a, b):
    return lax.dot_general(a, b, (((1,), (1,)), ((), ())), preferred_element_type=F32)


def _diff_kernel(q_ref, k_ref, v_ref, lam_ref, g_ref, o_ref, vt_ref, *, seq, lam_init):
    lf = lam_ref[...]
    lam = (jnp.exp(jnp.sum(lf[0:1] * lf[1:2], axis=-1, keepdims=True))
           - jnp.exp(jnp.sum(lf[2:3] * lf[3:4], axis=-1, keepdims=True)) + lam_init)
    hd = 2 * A_HEAD_DIM
    k = k_ref[...]
    vt_ref[0:hd, :] = v_ref[...].astype(F32).T.astype(BF16)
    vt_ref[hd:, :] = jnp.ones((vt_ref.shape[0] - hd, seq), BF16)
    vt = vt_ref[...]
    g = g_ref[...] * (1.0 - lam_init)
    lane = lax.broadcasted_iota(jnp.int32, (DIFF_TQ, LANES), 1)
    first = lane < A_HEAD_DIM

    def scores(t):
        q = q_ref[t * DIFF_TQ:(t + 1) * DIFF_TQ, :]
        zero = jnp.zeros_like(q)
        return (_nt_dot(k, jnp.where(first, q, zero)),
                _nt_dot(k, jnp.where(first, zero, q)))

    nblk = seq // DIFF_TQ
    s_next = scores(0)
    for t in range(nblk):
        s1, s2 = s_next
        if t + 1 < nblk:
            s_next = scores(t + 1)
        e1 = jnp.exp(s1 - jnp.max(s1, axis=0, keepdims=True)).astype(BF16)
        e2 = jnp.exp(s2 - jnp.max(s2, axis=0, keepdims=True)).astype(BF16)
        u1 = jnp.dot(vt, e1, preferred_element_type=F32)
        u2 = jnp.dot(vt, e2, preferred_element_type=F32)
        o = u1[:hd] * (1.0 / u1[hd:hd + 1]) - u2[:hd] * (lam / u2[hd:hd + 1])
        ms = jnp.mean(o * o, axis=0, keepdims=True)
        o_ref[t * DIFF_TQ:(t + 1) * DIFF_TQ, :] = (
            (o * lax.rsqrt(ms + NORM_EPS) * g).T.astype(o_ref.dtype))


def _diff_attention(z, lam_all, subln_all, layer, seq):
    t = z.shape[0]
    kb = A_WIDTH // LANES
    hd = 2 * A_HEAD_DIM
    blk = lambda off: pl.BlockSpec((seq, LANES), lambda b, h: (b, off + h))
    return pl.pallas_call(
        functools.partial(_diff_kernel, seq=seq, lam_init=0.8 - 0.6 * math.exp(-0.3 * layer)),
        grid=(t // seq, A_HEADS),
        in_specs=[blk(0), blk(kb), blk(2 * kb),
                  pl.BlockSpec((None, 4, A_HEAD_DIM), lambda b, h: (layer, 0, 0)),
                  pl.BlockSpec((None, hd, 1), lambda b, h: (layer, 0, 0))],
        out_specs=pl.BlockSpec((seq, LANES), lambda b, h: (b, h)),
        out_shape=jax.ShapeDtypeStruct((t, A_WIDTH), BF16),
        scratch_shapes=[pltpu.VMEM((hd + BF16_ROWS, seq), BF16)],
        compiler_params=_params("parallel", "parallel"),
        name="diff_attention",
    )(z, z, z, lam_all, subln_all)


def _dil_kernel(*refs, seq):
    q_refs, k_refs, v_refs = refs[0:3], refs[3:6], refs[6:9]
    o_ref = refs[9]
    o_scr, l_scr = refs[10:13], refs[13:16]
    scale = B_HEAD_DIM ** -0.5
    row = lax.broadcasted_iota(jnp.int32, (DIL_TQ, DIL_WIN), 0)
    col = lax.broadcasted_iota(jnp.int32, (DIL_TQ, DIL_WIN), 1)
    rel = col - row

    def step(t, carry):
        q0 = pl.multiple_of(t * DIL_TQ, DIL_TQ)
        ws = pl.multiple_of(jnp.clip(q0 - DIL_HALF, 0, seq - DIL_WIN), DIL_HALF)
        band = jnp.abs(rel + (ws - q0)) <= DIL_HALF
        for gi, (window, d) in enumerate(B_PATTERNS):
            assert window // (2 * d) == DIL_HALF
            n_sub = seq // d
            nblk = n_sub // DIL_TQ
            p = t // nblk
            n = t % nblk
            lo = p * n_sub - ws
            q = q_refs[gi][pl.ds(q0, DIL_TQ), :]
            k = k_refs[gi][pl.ds(ws, DIL_WIN), :]
            v = v_refs[gi][pl.ds(ws, DIL_WIN), :]
            s = _nt_dot(q, k) * scale
            ok = band & (col >= lo) & (col < lo + n_sub) if d > 1 else band
            s = jnp.where(ok, s, MASK_VALUE)
            m = jnp.max(s, axis=-1, keepdims=True)
            e = jnp.exp(s - m)
            l = jnp.sum(e, axis=-1, keepdims=True)
            o = jnp.dot(e.astype(v.dtype), v, preferred_element_type=F32) / l
            lse = jnp.broadcast_to(m + jnp.log(l), (DIL_TQ, LANES))
            dst = pl.ds(n * (DIL_TQ * d) + p, DIL_TQ, stride=d) if d > 1 else pl.ds(q0, DIL_TQ)
            o_scr[gi][dst, :] = o
            l_scr[gi][dst, :] = lse
        return carry

    lax.fori_loop(0, seq // DIL_TQ, step, 0, unroll=DIL_UNROLL)

    l0, l1, l2 = l_scr[0][...], l_scr[1][...], l_scr[2][...]
    m = jnp.maximum(jnp.maximum(l0, l1), l2)
    e0, e1, e2 = jnp.exp(l0 - m), jnp.exp(l1 - m), jnp.exp(l2 - m)
    num = e0 * o_scr[0][...] + e1 * o_scr[1][...] + e2 * o_scr[2][...]
    o_ref[...] = (num / (e0 + e1 + e2)).astype(o_ref.dtype)


def _dil_attention(z, seq):
    t = z.shape[0]
    base = 3 * A_WIDTH // LANES
    per = B_WIDTH // LANES

    def blk(which, g):
        off = base + which * per + g * B_HEADS
        return pl.BlockSpec((seq, LANES), lambda b, h: (b, off + h))

    in_specs = [blk(w, g) for w in range(3) for g in range(B_GROUPS)]
    return pl.pallas_call(
        functools.partial(_dil_kernel, seq=seq),
        grid=(t // seq, B_HEADS),
        in_specs=in_specs,
        out_specs=pl.BlockSpec((seq, LANES), lambda b, h: (b, h)),
        out_shape=jax.ShapeDtypeStruct((t, B_OUT), BF16),
        scratch_shapes=[pltpu.VMEM((seq, LANES), F32)] * (2 * B_GROUPS),
        compiler_params=_params("parallel", "parallel"),
        name="dilated_attention",
    )(*([z] * 9))


def _mix_kernel(oa_ref, ob_ref, ga_ref, gb_ref, x_ref, wpa_ref, wpb_ref, wo_ref, g_ref,
                xo_ref, hn_ref):
    ya = jnp.dot(oa_ref[...], wpa_ref[...], preferred_element_type=F32)
    yb = jnp.dot(ob_ref[...], wpb_ref[...], preferred_element_type=F32)
    merged = (ga_ref[...] * ya + gb_ref[...] * yb).astype(BF16)
    xo = x_ref[...] + jnp.dot(merged, wo_ref[...], preferred_element_type=F32)
    xo_ref[...] = xo
    hn_ref[...] = _rms(xo, g_ref[...]).astype(hn_ref.dtype)


def _mix_out(oa, ob, gates, x, w_pa, w_pb, w_out, norm_ffn, layer):
    t = x.shape[0]
    row = lambda width, col=0: pl.BlockSpec((MIX_TM, width), lambda i: (i, col))
    whole = lambda k, n: pl.BlockSpec((None, k, n), lambda i: (layer, 0, 0),
                                      pipeline_mode=pl.Buffered(1))
    return pl.pallas_call(
        _mix_kernel,
        grid=(t // MIX_TM,),
        in_specs=[row(A_WIDTH), row(B_OUT), row(D_MODEL, 0), row(D_MODEL, 1), row(D_MODEL),
                  whole(A_WIDTH, D_MODEL), whole(B_OUT, D_MODEL), whole(D_MODEL, D_MODEL),
                  pl.BlockSpec((None, 1, D_MODEL), lambda i: (layer, 0, 0))],
        out_specs=[row(D_MODEL), row(D_MODEL)],
        out_shape=[jax.ShapeDtypeStruct((t, D_MODEL), F32),
                   jax.ShapeDtypeStruct((t, D_MODEL), BF16)],
        compiler_params=_params("parallel"),
        name="mix_out",
    )(oa, ob, gates, gates, x, w_pa, w_pb, w_out, norm_ffn)


def _ffn_kernel(hn_ref, wg_ref, wu_ref, w2_ref, x_ref, g_ref, *rest, emit_x):
    outs, acc_ref = rest[:-1], rest[-1]
    k = pl.program_id(1)
    hn = hn_ref[...]
    hg = jnp.dot(hn, wg_ref[...], preferred_element_type=F32)
    hu = jnp.dot(hn, wu_ref[...], preferred_element_type=F32)
    act = (hg / (1.0 + jnp.exp(-hg)) * hu).astype(BF16)
    part = jnp.dot(act, w2_ref[...], preferred_element_type=F32)

    @pl.when(k == 0)
    def _():
        acc_ref[...] = x_ref[...] + part

    @pl.when(k > 0)
    def _():
        acc_ref[...] += part

    @pl.when(k == pl.num_programs(1) - 1)
    def _():
        xo = acc_ref[...]
        if emit_x:
            outs[0][...] = xo
        outs[-1][...] = _rms(xo, g_ref[...]).astype(outs[-1].dtype)


def _ffn(hn, w_ffn_in, w_ffn_out, x, norm_all, layer, norm_layer, emit_x):
    t = x.shape[0]
    nk = D_FF // FFN_TF
    row = pl.BlockSpec((FFN_TM, D_MODEL), lambda i, k: (i, 0))
    if norm_all.ndim == 3:
        g_spec = pl.BlockSpec((None, 1, D_MODEL), lambda i, k: (norm_layer, 0, 0))
    else:
        g_spec = pl.BlockSpec((1, D_MODEL), lambda i, k: (0, 0))
    out_specs = [row]
    out_shape = [jax.ShapeDtypeStruct((t, D_MODEL), BF16 if emit_x else F32)]
    if emit_x:
        out_specs = [row, row]
        out_shape = [jax.ShapeDtypeStruct((t, D_MODEL), F32)] + out_shape
    return pl.pallas_call(
        functools.partial(_ffn_kernel, emit_x=emit_x),
        grid=(t // FFN_TM, nk),
        in_specs=[row,
                  pl.BlockSpec((None, D_MODEL, FFN_TF), lambda i, k: (layer, 0, k)),
                  pl.BlockSpec((None, D_MODEL, FFN_TF), lambda i, k: (layer, 0, nk + k)),
                  pl.BlockSpec((None, FFN_TF, D_MODEL), lambda i, k: (layer, k, 0)),
                  row, g_spec],
        out_specs=out_specs,
        out_shape=out_shape,
        scratch_shapes=[pltpu.VMEM((FFN_TM, D_MODEL), F32)],
        compiler_params=_params("parallel", "arbitrary"),
        name="ffn",
    )(hn, w_ffn_in, w_ffn_in, w_ffn_out, x, norm_all)


def _rope_lane_tables(seq, head_dim):
    rot = head_dim // ROPE_FRAC
    inv = ROPE_THETA ** (-(jnp.arange(0, rot, 2, dtype=F32) / rot))
    ang = jnp.arange(seq, dtype=F32)[:, None] * inv[None, :]
    cos, sin = jnp.cos(ang), jnp.sin(ang)
    rest = head_dim - rot
    c = jnp.concatenate([cos, cos, jnp.ones((seq, rest), F32)], axis=-1)
    s = jnp.concatenate([-sin, sin, jnp.zeros((seq, rest), F32)], axis=-1)
    reps = LANES // head_dim
    return jnp.tile(c, (1, reps)), jnp.tile(s, (1, reps))


def _trunk(x, seq, norm_mix, norm_ffn, w_in, gate_bias, diff_lambda, diff_subln,
           w_pa, w_pb, w_out, w_ffn_in, w_ffn_out, norm_final):
    tables = _rope_lane_tables(seq, A_HEAD_DIM) + _rope_lane_tables(seq, B_HEAD_DIM)
    xn = _rmsnorm(x, norm_mix, 0)
    for layer in range(DEPTH):
        z = _qkv_proj(xn, w_in, layer, tables, seq)
        gates = _gate_proj(xn, w_in, gate_bias, layer, seq)
        oa = _diff_attention(z, diff_lambda, diff_subln, layer, seq)
        ob = _dil_attention(z, seq)
        x, hn = _mix_out(oa, ob, gates, x, w_pa, w_pb, w_out, norm_ffn, layer)
        if layer + 1 < DEPTH:
            x, xn = _ffn(hn, w_ffn_in, w_ffn_out, x, norm_mix, layer, layer + 1, True)
        else:
            (y,) = _ffn(hn, w_ffn_in, w_ffn_out, x, norm_final, layer, 0, False)
    return y


def kernel(x_prompt, x_sample, norm_mix, norm_ffn, w_in, gate_bias, diff_lambda, diff_subln,
           w_proj_a, w_proj_b, w_out, w_ffn_in, w_ffn_out, norm_final):
    bp, seq, d = x_prompt.shape
    bs = x_sample.shape[0]
    assert x_sample.shape[1:] == (seq, d) and d == D_MODEL
    assert seq % (B_PATTERNS[-1][1] * DIL_TQ) == 0 and seq >= DIL_WIN
    x = jnp.concatenate([x_prompt.reshape(bp * seq, d), x_sample.reshape(bs * seq, d)], axis=0)
    y = _trunk(
        x, seq,
        norm_mix.reshape(DEPTH, 1, D_MODEL), norm_ffn.reshape(DEPTH, 1, D_MODEL),
        w_in.astype(BF16), gate_bias.reshape(DEPTH, 1, GATE_WIDTH),
        diff_lambda, diff_subln.reshape(DEPTH, 2 * A_HEAD_DIM, 1),
        w_proj_a.astype(BF16), w_proj_b.astype(BF16), w_out.astype(BF16),
        w_ffn_in.astype(BF16), w_ffn_out.astype(BF16), norm_final.reshape(1, D_MODEL))
    return (y[:bp * seq].reshape(bp, seq, d), y[bp * seq:].reshape(bs, seq, d))
```

```python
import functools
import math

import jax
import jax.numpy as jnp
from jax import lax
from jax.experimental import pallas as pl
from jax.experimental.pallas import tpu as pltpu

D_MODEL = 2048
DEPTH = 4
A_HEADS = 8
A_HEAD_DIM = 64
A_WIDTH = A_HEADS * 2 * A_HEAD_DIM
B_PATTERNS = ((128, 1), (512, 4), (2048, 16))
B_GROUPS = len(B_PATTERNS)
B_HEADS = 4
B_HEAD_DIM = 128
B_WIDTH = B_GROUPS * B_HEADS * B_HEAD_DIM
B_OUT = B_HEADS * B_HEAD_DIM
QKV_WIDTH = 3 * A_WIDTH + 3 * B_WIDTH
GATE_WIDTH = 2 * D_MODEL
D_FF = 5632
ROPE_THETA = 500000.0
ROPE_FRAC = 4
NORM_EPS = 1e-6
MASK_VALUE = -1e30

LANES = 128
BF16_ROWS = 16
MXU_WIDTH = 256
VMEM_LIMIT = 60 * 1024 * 1024

PROJ_TN = 512
DIFF_TQ = 256
DIL_UNROLL = 4
DIL_TQ = 128
DIL_HALF = 64
DIL_WIN = 256
MIX_TM = 512
FFN_TM = 512
FFN_TF = 512
FFN_CHUNKS = 2
NORM_TM = 512

BF16 = jnp.bfloat16
F32 = jnp.float32


def _params(*sem):
    return pltpu.CompilerParams(dimension_semantics=sem, vmem_limit_bytes=VMEM_LIMIT)


def _sigmoid(x):
    return 0.5 * jnp.tanh(0.5 * x) + 0.5


def _rms(x, g):
    ms = jnp.mean(x * x, axis=-1, keepdims=True)
    return x * lax.rsqrt(ms + NORM_EPS) * g


def _rmsnorm_kernel(x_ref, g_ref, o_ref):
    o_ref[...] = _rms(x_ref[...], g_ref[...]).astype(o_ref.dtype)


def _rmsnorm(x, g_all, layer):
    t = x.shape[0]
    return pl.pallas_call(
        _rmsnorm_kernel,
        grid=(t // NORM_TM,),
        in_specs=[pl.BlockSpec((NORM_TM, D_MODEL), lambda i: (i, 0)),
                  pl.BlockSpec((None, 1, D_MODEL), lambda i: (layer, 0, 0))],
        out_specs=pl.BlockSpec((NORM_TM, D_MODEL), lambda i: (i, 0)),
        out_shape=jax.ShapeDtypeStruct((t, D_MODEL), BF16),
        compiler_params=_params("parallel"),
        name="rmsnorm_in",
    )(x, g_all)


def _rope_chunk(x, c, s, head_dim):
    half = head_dim // ROPE_FRAC // 2
    lane = lax.broadcasted_iota(jnp.int32, x.shape, 1) % head_dim
    up = pltpu.roll(x, LANES - half, 1)
    down = pltpu.roll(x, half, 1)
    return x * c + jnp.where(lane < half, up, down) * s


def _qkv_kernel(x_ref, w_ref, ca_ref, sa_ref, cb_ref, sb_ref, o_ref, *acc_refs, seq):
    j = pl.program_id(1)
    n_a = 2 * A_WIDTH // PROJ_TN
    n_av = 3 * A_WIDTH // PROJ_TN
    n_b = B_WIDTH // PROJ_TN
    assert n_b == B_GROUPS

    def tile(rope, d):
        n_sub = seq // d
        x = x_ref[...]
        for m in range(PROJ_TN // MXU_WIDTH):
            acc = jnp.dot(x, w_ref[:, m * MXU_WIDTH:(m + 1) * MXU_WIDTH], preferred_element_type=F32)
            for c in range(MXU_WIDTH // LANES):
                val = acc[:, c * LANES:(c + 1) * LANES]
                chunk = m * (MXU_WIDTH // LANES) + c
                sl = slice(chunk * LANES, (chunk + 1) * LANES)
                if rope == "a":
                    scale = jnp.where(j < n_a // 2, A_HEAD_DIM ** -0.5, 1.0).astype(F32)
                    val = _rope_chunk(val, ca_ref[...], sa_ref[...], A_HEAD_DIM) * scale
                elif rope == "b":
                    val = _rope_chunk(val, cb_ref[...], sb_ref[...], B_HEAD_DIM)
                if d == 1:
                    o_ref[:, sl] = val.astype(o_ref.dtype)
                else:
                    ref = acc_refs[chunk]
                    ref[...] = val
                    for p in range(d):
                        o_ref[p * n_sub:(p + 1) * n_sub, sl] = (
                            ref[pl.ds(p, n_sub, stride=d), :].astype(o_ref.dtype))

    pl.when(j < n_a)(lambda: tile("a", 1))
    pl.when((j >= n_a) & (j < n_av))(lambda: tile(None, 1))
    group = (j - n_av) % n_b
    for gi, (_, d) in enumerate(B_PATTERNS):
        is_group = (j >= n_av) & (group == gi)
        pl.when(is_group & (j < n_av + 2 * n_b))(functools.partial(tile, "b", d))
        pl.when(is_group & (j >= n_av + 2 * n_b))(functools.partial(tile, None, d))


def _qkv_proj(xn, w_in, layer, tables, seq):
    t = xn.shape[0]
    tab_spec = pl.BlockSpec((seq, LANES), lambda i, j: (0, 0))
    return pl.pallas_call(
        functools.partial(_qkv_kernel, seq=seq),
        grid=(t // seq, QKV_WIDTH // PROJ_TN),
        in_specs=[pl.BlockSpec((seq, D_MODEL), lambda i, j: (i, 0)),
                  pl.BlockSpec((None, D_MODEL, PROJ_TN), lambda i, j: (layer, 0, j)),
                  tab_spec, tab_spec, tab_spec, tab_spec],
        out_specs=pl.BlockSpec((seq, PROJ_TN), lambda i, j: (i, j)),
        out_shape=jax.ShapeDtypeStruct((t, QKV_WIDTH), BF16),
        scratch_shapes=[pltpu.VMEM((seq, LANES), F32)] * (PROJ_TN // LANES),
        compiler_params=_params("parallel", "arbitrary"),
        name="qkv_proj",
    )(xn, w_in, *tables)


def _gate_kernel(x_ref, w_ref, b_ref, o_ref):
    x = x_ref[...]
    for m in range(PROJ_TN // MXU_WIDTH):
        cols = slice(m * MXU_WIDTH, (m + 1) * MXU_WIDTH)
        z = jnp.dot(x, w_ref[:, cols], preferred_element_type=F32) + b_ref[:, cols]
        o_ref[:, cols] = _sigmoid(z)


def _gate_proj(xn, w_in, bias, layer, seq):
    t = xn.shape[0]
    col0 = QKV_WIDTH // PROJ_TN
    return pl.pallas_call(
        _gate_kernel,
        grid=(t // seq, GATE_WIDTH // PROJ_TN),
        in_specs=[pl.BlockSpec((seq, D_MODEL), lambda i, j: (i, 0)),
                  pl.BlockSpec((None, D_MODEL, PROJ_TN), lambda i, j: (layer, 0, col0 + j)),
                  pl.BlockSpec((None, 1, PROJ_TN), lambda i, j: (layer, 0, j))],
        out_specs=pl.BlockSpec((seq, PROJ_TN), lambda i, j: (i, j)),
        out_shape=jax.ShapeDtypeStruct((t, GATE_WIDTH), F32),
        compiler_params=_params("parallel", "arbitrary"),
        name="gate_proj",
    )(xn, w_in, bias)


def _nt_dot(a, b):
    return lax.dot_general(a, b, (((1,), (1,)), ((), ())), preferred_element_type=F32)


def _diff_kernel(q_ref, k_ref, v_ref, lam_ref, g_ref, o_ref, vt_ref, *, seq, lam_init):
    lf = lam_ref[...]
    lam = (jnp.exp(jnp.sum(lf[0:1] * lf[1:2], axis=-1, keepdims=True))
           - jnp.exp(jnp.sum(lf[2:3] * lf[3:4], axis=-1, keepdims=True)) + lam_init)
    hd = 2 * A_HEAD_DIM
    k = k_ref[...]
    vt_ref[0:hd, :] = v_ref[...].astype(F32).T.astype(BF16)
    vt_ref[hd:, :] = jnp.ones((vt_ref.shape[0] - hd, seq), BF16)
    vt = vt_ref[...]
    g = g_ref[...] * (1.0 - lam_init)
    lane = lax.broadcasted_iota(jnp.int32, (DIFF_TQ, LANES), 1)
    first = lane < A_HEAD_DIM

    def scores(t):
        q = q_ref[t * DIFF_TQ:(t + 1) * DIFF_TQ, :]
        zero = jnp.zeros_like(q)
        return (_nt_dot(k, jnp.where(first, q, zero)),
                _nt_dot(k, jnp.where(first, zero, q)))

    nblk = seq // DIFF_TQ
    s_next = scores(0)
    for t in range(nblk):
        s1, s2 = s_next
        if t + 1 < nblk:
            s_next = scores(t + 1)
        e1 = jnp.exp(s1 - jnp.max(s1, axis=0, keepdims=True)).astype(BF16)
        e2 = jnp.exp(s2 - jnp.max(s2, axis=0, keepdims=True)).astype(BF16)
        u1 = jnp.dot(vt, e1, preferred_element_type=F32)
        u2 = jnp.dot(vt, e2, preferred_element_type=F32)
        o = u1[:hd] * (1.0 / u1[hd:hd + 1]) - u2[:hd] * (lam / u2[hd:hd + 1])
        ms = jnp.mean(o * o, axis=0, keepdims=True)
        o_ref[t * DIFF_TQ:(t + 1) * DIFF_TQ, :] = (
            (o * lax.rsqrt(ms + NORM_EPS) * g).T.astype(o_ref.dtype))


def _diff_attention(z, lam_all, subln_all, layer, seq):
    t = z.shape[0]
    kb = A_WIDTH // LANES
    hd = 2 * A_HEAD_DIM
    blk = lambda off: pl.BlockSpec((seq, LANES), lambda b, h: (b, off + h))
    return pl.pallas_call(
        functools.partial(_diff_kernel, seq=seq, lam_init=0.8 - 0.6 * math.exp(-0.3 * layer)),
        grid=(t // seq, A_HEADS),
        in_specs=[blk(0), blk(kb), blk(2 * kb),
                  pl.BlockSpec((None, 4, A_HEAD_DIM), lambda b, h: (layer, 0, 0)),
                  pl.BlockSpec((None, hd, 1), lambda b, h: (layer, 0, 0))],
        out_specs=pl.BlockSpec((seq, LANES), lambda b, h: (b, h)),
        out_shape=jax.ShapeDtypeStruct((t, A_WIDTH), BF16),
        scratch_shapes=[pltpu.VMEM((hd + BF16_ROWS, seq), BF16)],
        compiler_params=_params("parallel", "parallel"),
        name="diff_attention",
    )(z, z, z, lam_all, subln_all)


def _dil_kernel(*refs, seq):
    q_refs, k_refs, v_refs = refs[0:3], refs[3:6], refs[6:9]
    o_ref = refs[9]
    o_scr, l_scr = refs[10:13], refs[13:16]
    scale = B_HEAD_DIM ** -0.5
    row = lax.broadcasted_iota(jnp.int32, (DIL_TQ, DIL_WIN), 0)
    col = lax.broadcasted_iota(jnp.int32, (DIL_TQ, DIL_WIN), 1)
    rel = col - row

    def step(t, carry):
        q0 = pl.multiple_of(t * DIL_TQ, DIL_TQ)
        ws = pl.multiple_of(jnp.clip(q0 - DIL_HALF, 0, seq - DIL_WIN), DIL_HALF)
        band = jnp.abs(rel + (ws - q0)) <= DIL_HALF
        for gi, (window, d) in enumerate(B_PATTERNS):
            assert window // (2 * d) == DIL_HALF
            n_sub = seq // d
            nblk = n_sub // DIL_TQ
            p = t // nblk
            n = t % nblk
            lo = p * n_sub - ws
            q = q_refs[gi][pl.ds(q0, DIL_TQ), :]
            k = k_refs[gi][pl.ds(ws, DIL_WIN), :]
            v = v_refs[gi][pl.ds(ws, DIL_WIN), :]
            s = _nt_dot(q, k) * scale
            ok = band & (col >= lo) & (col < lo + n_sub) if d > 1 else band
            s = jnp.where(ok, s, MASK_VALUE)
            m = jnp.max(s, axis=-1, keepdims=True)
            e = jnp.exp(s - m)
            l = jnp.sum(e, axis=-1, keepdims=True)
            o = jnp.dot(e.astype(v.dtype), v, preferred_element_type=F32) / l
            lse = jnp.broadcast_to(m + jnp.log(l), (DIL_TQ, LANES))
            dst = pl.ds(n * (DIL_TQ * d) + p, DIL_TQ, stride=d) if d > 1 else pl.ds(q0, DIL_TQ)
            o_scr[gi][dst, :] = o
            l_scr[gi][dst, :] = lse
        return carry

    lax.fori_loop(0, seq // DIL_TQ, step, 0, unroll=DIL_UNROLL)

    l0, l1, l2 = l_scr[0][...], l_scr[1][...], l_scr[2][...]
    m = jnp.maximum(jnp.maximum(l0, l1), l2)
    e0, e1, e2 = jnp.exp(l0 - m), jnp.exp(l1 - m), jnp.exp(l2 - m)
    num = e0 * o_scr[0][...] + e1 * o_scr[1][...] + e2 * o_scr[2][...]
    o_ref[...] = (num / (e0 + e1 + e2)).astype(o_ref.dtype)


def _dil_attention(z, seq):
    t = z.shape[0]
    base = 3 * A_WIDTH // LANES
    per = B_WIDTH // LANES

    def blk(which, g):
        off = base + which * per + g * B_HEADS
        return pl.BlockSpec((seq, LANES), lambda b, h: (b, off + h))

    in_specs = [blk(w, g) for w in range(3) for g in range(B_GROUPS)]
    return pl.pallas_call(
        functools.partial(_dil_kernel, seq=seq),
        grid=(t // seq, B_HEADS),
        in_specs=in_specs,
        out_specs=pl.BlockSpec((seq, LANES), lambda b, h: (b, h)),
        out_shape=jax.ShapeDtypeStruct((t, B_OUT), BF16),
        scratch_shapes=[pltpu.VMEM((seq, LANES), F32)] * (2 * B_GROUPS),
        compiler_params=_params("parallel", "parallel"),
        name="dilated_attention",
    )(*([z] * 9))


def _mix_kernel(oa_ref, ob_ref, ga_ref, gb_ref, x_ref, wpa_ref, wpb_ref, wo_ref, g_ref,
                xo_ref, hn_ref):
    ya = jnp.dot(oa_ref[...], wpa_ref[...], preferred_element_type=F32)
    yb = jnp.dot(ob_ref[...], wpb_ref[...], preferred_element_type=F32)
    merged = (ga_ref[...] * ya + gb_ref[...] * yb).astype(BF16)
    xo = x_ref[...] + jnp.dot(merged, wo_ref[...], preferred_element_type=F32)
    xo_ref[...] = xo
    hn_ref[...] = _rms(xo, g_ref[...]).astype(hn_ref.dtype)


def _mix_out(oa, ob, gates, x, w_pa, w_pb, w_out, norm_ffn, layer):
    t = x.shape[0]
    row = lambda width, col=0: pl.BlockSpec((MIX_TM, width), lambda i: (i, col))
    whole = lambda k, n: pl.BlockSpec((None, k, n), lambda i: (layer, 0, 0),
                                      pipeline_mode=pl.Buffered(1))
    return pl.pallas_call(
        _mix_kernel,
        grid=(t // MIX_TM,),
        in_specs=[row(A_WIDTH), row(B_OUT), row(D_MODEL, 0), row(D_MODEL, 1), row(D_MODEL),
                  whole(A_WIDTH, D_MODEL), whole(B_OUT, D_MODEL), whole(D_MODEL, D_MODEL),
                  pl.BlockSpec((None, 1, D_MODEL), lambda i: (layer, 0, 0))],
        out_specs=[row(D_MODEL), row(D_MODEL)],
        out_shape=[jax.ShapeDtypeStruct((t, D_MODEL), F32),
                   jax.ShapeDtypeStruct((t, D_MODEL), BF16)],
        compiler_params=_params("parallel"),
        name="mix_out",
    )(oa, ob, gates, gates, x, w_pa, w_pb, w_out, norm_ffn)


def _ffn_kernel(hn_ref, wg_ref, wu_ref, w2_ref, x_ref, g_ref, *rest, emit_x):
    outs, acc_ref = rest[:-1], rest[-1]
    k = pl.program_id(1)

    @pl.when(k == 0)
    def _():
        acc_ref[...] = x_ref[...]

    hn = hn_ref[...]
    width = FFN_TF // FFN_CHUNKS
    acts = []
    for c in range(FFN_CHUNKS):
        cols = slice(c * width, (c + 1) * width)
        hg = jnp.dot(hn, wg_ref[:, cols], preferred_element_type=F32)
        hu = jnp.dot(hn, wu_ref[:, cols], preferred_element_type=F32)
        acts.append((hg * _sigmoid(hg) * hu).astype(BF16))
    part = None
    for c in range(FFN_CHUNKS):
        rows = slice(c * width, (c + 1) * width)
        p = jnp.dot(acts[c], w2_ref[rows, :], preferred_element_type=F32)
        part = p if part is None else part + p
    acc_ref[...] += part

    @pl.when(k == pl.num_programs(1) - 1)
    def _():
        xo = acc_ref[...]
        if emit_x:
            outs[0][...] = xo
        outs[-1][...] = _rms(xo, g_ref[...]).astype(outs[-1].dtype)


def _ffn(hn, w_ffn_in, w_ffn_out, x, norm_all, layer, norm_layer, emit_x):
    t = x.shape[0]
    nk = D_FF // FFN_TF
    row = pl.BlockSpec((FFN_TM, D_MODEL), lambda i, k: (i, 0))
    if norm_all.ndim == 3:
        g_spec = pl.BlockSpec((None, 1, D_MODEL), lambda i, k: (norm_layer, 0, 0))
    else:
        g_spec = pl.BlockSpec((1, D_MODEL), lambda i, k: (0, 0))
    out_specs = [row]
    out_shape = [jax.ShapeDtypeStruct((t, D_MODEL), BF16 if emit_x else F32)]
    if emit_x:
        out_specs = [row, row]
        out_shape = [jax.ShapeDtypeStruct((t, D_MODEL), F32)] + out_shape
    return pl.pallas_call(
        functools.partial(_ffn_kernel, emit_x=emit_x),
        grid=(t // FFN_TM, nk),
        in_specs=[row,
                  pl.BlockSpec((None, D_MODEL, FFN_TF), lambda i, k: (layer, 0, k)),
                  pl.BlockSpec((None, D_MODEL, FFN_TF), lambda i, k: (layer, 0, nk + k)),
                  pl.BlockSpec((None, FFN_TF, D_MODEL), lambda i, k: (layer, k, 0)),
                  row, g_spec],
        out_specs=out_specs,
        out_shape=out_shape,
        scratch_shapes=[pltpu.VMEM((FFN_TM, D_MODEL), F32)],
        compiler_params=_params("parallel", "arbitrary"),
        name="ffn",
    )(hn, w_ffn_in, w_ffn_in, w_ffn_out, x, norm_all)


def _rope_lane_tables(seq, head_dim):
    rot = head_dim // ROPE_FRAC
    inv = ROPE_THETA ** (-(jnp.arange(0, rot, 2, dtype=F32) / rot))
    ang = jnp.arange(seq, dtype=F32)[:, None] * inv[None, :]
    cos, sin = jnp.cos(ang), jnp.sin(ang)
    rest = head_dim - rot
    c = jnp.concatenate([cos, cos, jnp.ones((seq, rest), F32)], axis=-1)
    s = jnp.concatenate([-sin, sin, jnp.zeros((seq, rest), F32)], axis=-1)
    reps = LANES // head_dim
    return jnp.tile(c, (1, reps)), jnp.tile(s, (1, reps))


def _trunk(x, seq, norm_mix, norm_ffn, w_in, gate_bias, diff_lambda, diff_subln,
           w_pa, w_pb, w_out, w_ffn_in, w_ffn_out, norm_final):
    tables = _rope_lane_tables(seq, A_HEAD_DIM) + _rope_lane_tables(seq, B_HEAD_DIM)
    xn = _rmsnorm(x, norm_mix, 0)
    for layer in range(DEPTH):
        z = _qkv_proj(xn, w_in, layer, tables, seq)
        gates = _gate_proj(xn, w_in, gate_bias, layer, seq)
        oa = _diff_attention(z, diff_lambda, diff_subln, layer, seq)
        ob = _dil_attention(z, seq)
        x, hn = _mix_out(oa, ob, gates, x, w_pa, w_pb, w_out, norm_ffn, layer)
        if layer + 1 < DEPTH:
            x, xn = _ffn(hn, w_ffn_in, w_ffn_out, x, norm_mix, layer, layer + 1, True)
        else:
            (y,) = _ffn(hn, w_ffn_in, w_ffn_out, x, norm_final, layer, 0, False)
    return y


def kernel(x_prompt, x_sample, norm_mix, norm_ffn, w_in, gate_bias, diff_lambda, diff_subln,
           w_proj_a, w_proj_b, w_out, w_ffn_in, w_ffn_out, norm_final):
    bp, seq, d = x_prompt.shape
    bs = x_sample.shape[0]
    assert x_sample.shape[1:] == (seq, d) and d == D_MODEL
    assert seq % (B_PATTERNS[-1][1] * DIL_TQ) == 0 and seq >= DIL_WIN
    x = jnp.concatenate([x_prompt.reshape(bp * seq, d), x_sample.reshape(bs * seq, d)], axis=0)
    y = _trunk(
        x, seq,
        norm_mix.reshape(DEPTH, 1, D_MODEL), norm_ffn.reshape(DEPTH, 1, D_MODEL),
        w_in.astype(BF16), gate_bias.reshape(DEPTH, 1, GATE_WIDTH),
        diff_lambda, diff_subln.reshape(DEPTH, 2 * A_HEAD_DIM, 1),
        w_proj_a.astype(BF16), w_proj_b.astype(BF16), w_out.astype(BF16),
        w_ffn_in.astype(BF16), w_ffn_out.astype(BF16), norm_final.reshape(1, D_MODEL))
    return (y[:bp * seq].reshape(bp, seq, d), y[bp * seq:].reshape(bs, seq, d))
```

```python
import functools
import math

import jax
import jax.numpy as jnp
from jax import lax
from jax.experimental import pallas as pl
from jax.experimental.pallas import tpu as pltpu

D_MODEL = 2048
DEPTH = 4
A_HEADS = 8
A_HEAD_DIM = 64
A_WIDTH = A_HEADS * 2 * A_HEAD_DIM
B_PATTERNS = ((128, 1), (512, 4), (2048, 16))
B_GROUPS = len(B_PATTERNS)
B_HEADS = 4
B_HEAD_DIM = 128
B_WIDTH = B_GROUPS * B_HEADS * B_HEAD_DIM
B_OUT = B_HEADS * B_HEAD_DIM
QKV_WIDTH = 3 * A_WIDTH + 3 * B_WIDTH
GATE_WIDTH = 2 * D_MODEL
D_FF = 5632
ROPE_THETA = 500000.0
ROPE_FRAC = 4
NORM_EPS = 1e-6
MASK_VALUE = -1e30

LANES = 128
BF16_ROWS = 16
MXU_WIDTH = 256
VMEM_LIMIT = 60 * 1024 * 1024

PROJ_TN = 512
DIFF_TQ = 256
DIL_UNROLL = 4
DIL_TQ = 128
DIL_HALF = 64
DIL_WIN = 256
MIX_TM = 512
FFN_TM = 512
FFN_TF = 512
FFN_CHUNKS = 2
NORM_TM = 512

BF16 = jnp.bfloat16
F32 = jnp.float32


def _params(*sem):
    return pltpu.CompilerParams(dimension_semantics=sem, vmem_limit_bytes=VMEM_LIMIT)


def _sigmoid(x):
    return 0.5 * jnp.tanh(0.5 * x) + 0.5


def _rms(x, g):
    ms = jnp.mean(x * x, axis=-1, keepdims=True)
    return x * lax.rsqrt(ms + NORM_EPS) * g


def _rmsnorm_kernel(x_ref, g_ref, o_ref):
    o_ref[...] = _rms(x_ref[...], g_ref[...]).astype(o_ref.dtype)


def _rmsnorm(x, g_all, layer):
    t = x.shape[0]
    return pl.pallas_call(
        _rmsnorm_kernel,
        grid=(t // NORM_TM,),
        in_specs=[pl.BlockSpec((NORM_TM, D_MODEL), lambda i: (i, 0)),
                  pl.BlockSpec((None, 1, D_MODEL), lambda i: (layer, 0, 0))],
        out_specs=pl.BlockSpec((NORM_TM, D_MODEL), lambda i: (i, 0)),
        out_shape=jax.ShapeDtypeStruct((t, D_MODEL), BF16),
        compiler_params=_params("parallel"),
        name="rmsnorm_in",
    )(x, g_all)


def _rope_chunk(x, c, s, head_dim):
    half = head_dim // ROPE_FRAC // 2
    lane = lax.broadcasted_iota(jnp.int32, x.shape, 1) % head_dim
    up = pltpu.roll(x, LANES - half, 1)
    down = pltpu.roll(x, half, 1)
    return x * c + jnp.where(lane < half, up, down) * s


def _proj_kernel(x_ref, w_ref, *rest, seq, head_dim, d, tn, scaled_tiles):
    if head_dim:
        c_ref, s_ref, o_ref, *scr = rest
    else:
        o_ref, *scr = rest
    j = pl.program_id(1)
    n_sub = seq // d
    x = x_ref[...]
    for m in range(tn // MXU_WIDTH):
        acc = jnp.dot(x, w_ref[:, m * MXU_WIDTH:(m + 1) * MXU_WIDTH], preferred_element_type=F32)
        for c in range(MXU_WIDTH // LANES):
            val = acc[:, c * LANES:(c + 1) * LANES]
            chunk = m * (MXU_WIDTH // LANES) + c
            sl = slice(chunk * LANES, (chunk + 1) * LANES)
            if head_dim:
                val = _rope_chunk(val, c_ref[...], s_ref[...], head_dim)
            if scaled_tiles:
                val = val * jnp.where(j < scaled_tiles, A_HEAD_DIM ** -0.5, 1.0).astype(F32)
            if d == 1:
                o_ref[:, sl] = val.astype(o_ref.dtype)
            else:
                scr[chunk][...] = val
                for p in range(d):
                    o_ref[p * n_sub:(p + 1) * n_sub, sl] = (
                        scr[chunk][pl.ds(p, n_sub, stride=d), :].astype(o_ref.dtype))


def _proj_class(xn, w_in, layer, seq, name, tiles, tn, tables=None, head_dim=0, d=1, scaled_tiles=0):
    t = xn.shape[0]

    def w_map(i, j):
        tile = tiles[0]
        for idx in range(1, len(tiles)):
            tile = jnp.where(j >= idx, tiles[idx], tile)
        return (layer, 0, tile)

    in_specs = [pl.BlockSpec((seq, D_MODEL), lambda i, j: (i, 0)),
                pl.BlockSpec((None, D_MODEL, tn), w_map)]
    args = [xn, w_in]
    if head_dim:
        in_specs += [pl.BlockSpec((seq, LANES), lambda i, j: (0, 0))] * 2
        args += list(tables)
    return pl.pallas_call(
        functools.partial(_proj_kernel, seq=seq, head_dim=head_dim, d=d, tn=tn,
                          scaled_tiles=scaled_tiles),
        grid=(t // seq, len(tiles)),
        in_specs=in_specs,
        out_specs=pl.BlockSpec((seq, tn), lambda i, j: (i, j)),
        out_shape=jax.ShapeDtypeStruct((t, len(tiles) * tn), BF16),
        scratch_shapes=[pltpu.VMEM((seq, LANES), F32)] * (tn // LANES if d > 1 else 0),
        compiler_params=_params("parallel", "arbitrary"),
        name=name,
    )(*args)


def _qkv_proj(xn, w_in, layer, tables, seq):
    tab_a, tab_b = tables[:2], tables[2:]
    wide = 2 * PROJ_TN
    n_a = 3 * A_WIDTH // PROJ_TN
    n_b = B_WIDTH // PROJ_TN
    assert n_b == B_GROUPS and A_WIDTH == wide
    proj = functools.partial(_proj_class, xn, w_in, layer, seq)
    a_qk = proj("proj_a_qk", (0, 1), wide, tables=tab_a, head_dim=A_HEAD_DIM, scaled_tiles=1)
    v_plain = proj("proj_v_plain", (n_a - 2, n_a - 1, n_a + 2 * n_b), PROJ_TN)
    groups = []
    for gi, (_, d) in enumerate(B_PATTERNS):
        qk = proj(f"proj_b_qk_d{d}", (n_a + gi, n_a + n_b + gi), PROJ_TN,
                  tables=tab_b, head_dim=B_HEAD_DIM, d=d)
        v = None if d == 1 else proj(f"proj_b_v_d{d}", (n_a + 2 * n_b + gi,), PROJ_TN, d=d)
        groups.append((qk, v))
    return a_qk, v_plain, groups


def _gate_kernel(x_ref, w_ref, b_ref, o_ref):
    x = x_ref[...]
    for m in range(PROJ_TN // MXU_WIDTH):
        cols = slice(m * MXU_WIDTH, (m + 1) * MXU_WIDTH)
        z = jnp.dot(x, w_ref[:, cols], preferred_element_type=F32) + b_ref[:, cols]
        o_ref[:, cols] = _sigmoid(z)


def _gate_proj(xn, w_in, bias, layer, seq):
    t = xn.shape[0]
    col0 = QKV_WIDTH // PROJ_TN
    return pl.pallas_call(
        _gate_kernel,
        grid=(t // seq, GATE_WIDTH // PROJ_TN),
        in_specs=[pl.BlockSpec((seq, D_MODEL), lambda i, j: (i, 0)),
                  pl.BlockSpec((None, D_MODEL, PROJ_TN), lambda i, j: (layer, 0, col0 + j)),
                  pl.BlockSpec((None, 1, PROJ_TN), lambda i, j: (layer, 0, j))],
        out_specs=pl.BlockSpec((seq, PROJ_TN), lambda i, j: (i, j)),
        out_shape=jax.ShapeDtypeStruct((t, GATE_WIDTH), F32),
        compiler_params=_params("parallel", "arbitrary"),
        name="gate_proj",
    )(xn, w_in, bias)


def _nt_dot(a, b):
    return lax.dot_general(a, b, (((1,), (1,)), ((), ())), preferred_element_type=F32)


def _diff_kernel(q_ref, k_ref, v_ref, lam_ref, g_ref, o_ref, vt_ref, *, seq, lam_init):
    lf = lam_ref[...]
    lam = (jnp.exp(jnp.sum(lf[0:1] * lf[1:2], axis=-1, keepdims=True))
           - jnp.exp(jnp.sum(lf[2:3] * lf[3:4], axis=-1, keepdims=True)) + lam_init)
    hd = 2 * A_HEAD_DIM
    k = k_ref[...]
    vt_ref[0:hd, :] = v_ref[...].astype(F32).T.astype(BF16)
    vt_ref[hd:, :] = jnp.ones((vt_ref.shape[0] - hd, seq), BF16)
    vt = vt_ref[...]
    g = g_ref[...] * (1.0 - lam_init)
    lane = lax.broadcasted_iota(jnp.int32, (DIFF_TQ, LANES), 1)
    first = lane < A_HEAD_DIM

    def scores(t):
        q = q_ref[t * DIFF_TQ:(t + 1) * DIFF_TQ, :]
        zero = jnp.zeros_like(q)
        return (_nt_dot(k, jnp.where(first, q, zero)),
                _nt_dot(k, jnp.where(first, zero, q)))

    nblk = seq // DIFF_TQ
    s_next = scores(0)
    for t in range(nblk):
        s1, s2 = s_next
        if t + 1 < nblk:
            s_next = scores(t + 1)
        e1 = jnp.exp(s1 - jnp.max(s1, axis=0, keepdims=True)).astype(BF16)
        e2 = jnp.exp(s2 - jnp.max(s2, axis=0, keepdims=True)).astype(BF16)
        u1 = jnp.dot(vt, e1, preferred_element_type=F32)
        u2 = jnp.dot(vt, e2, preferred_element_type=F32)
        o = u1[:hd] * (1.0 / u1[hd:hd + 1]) - u2[:hd] * (lam / u2[hd:hd + 1])
        ms = jnp.mean(o * o, axis=0, keepdims=True)
        o_ref[t * DIFF_TQ:(t + 1) * DIFF_TQ, :] = (
            (o * lax.rsqrt(ms + NORM_EPS) * g).T.astype(o_ref.dtype))


def _diff_attention(a_qk, v_plain, lam_all, subln_all, layer, seq):
    t = a_qk.shape[0]
    kb = A_WIDTH // LANES
    hd = 2 * A_HEAD_DIM
    blk = lambda off: pl.BlockSpec((seq, LANES), lambda b, h: (b, off + h))
    return pl.pallas_call(
        functools.partial(_diff_kernel, seq=seq, lam_init=0.8 - 0.6 * math.exp(-0.3 * layer)),
        grid=(t // seq, A_HEADS),
        in_specs=[blk(0), blk(kb), blk(0),
                  pl.BlockSpec((None, 4, A_HEAD_DIM), lambda b, h: (layer, 0, 0)),
                  pl.BlockSpec((None, hd, 1), lambda b, h: (layer, 0, 0))],
        out_specs=pl.BlockSpec((seq, LANES), lambda b, h: (b, h)),
        out_shape=jax.ShapeDtypeStruct((t, A_WIDTH), BF16),
        scratch_shapes=[pltpu.VMEM((hd + BF16_ROWS, seq), BF16)],
        compiler_params=_params("parallel", "parallel"),
        name="diff_attention",
    )(a_qk, a_qk, v_plain, lam_all, subln_all)


def _dil_kernel(*refs, seq):
    q_refs, k_refs, v_refs = refs[0:3], refs[3:6], refs[6:9]
    o_ref = refs[9]
    o_scr, l_scr = refs[10:13], refs[13:16]
    scale = B_HEAD_DIM ** -0.5
    row = lax.broadcasted_iota(jnp.int32, (DIL_TQ, DIL_WIN), 0)
    col = lax.broadcasted_iota(jnp.int32, (DIL_TQ, DIL_WIN), 1)
    rel = col - row

    def step(t, carry):
        q0 = pl.multiple_of(t * DIL_TQ, DIL_TQ)
        ws = pl.multiple_of(jnp.clip(q0 - DIL_HALF, 0, seq - DIL_WIN), DIL_HALF)
        band = jnp.abs(rel + (ws - q0)) <= DIL_HALF
        for gi, (window, d) in enumerate(B_PATTERNS):
            assert window // (2 * d) == DIL_HALF
            n_sub = seq // d
            nblk = n_sub // DIL_TQ
            p = t // nblk
            n = t % nblk
            lo = p * n_sub - ws
            q = q_refs[gi][pl.ds(q0, DIL_TQ), :]
            k = k_refs[gi][pl.ds(ws, DIL_WIN), :]
            v = v_refs[gi][pl.ds(ws, DIL_WIN), :]
            s = _nt_dot(q, k) * scale
            ok = band & (col >= lo) & (col < lo + n_sub) if d > 1 else band
            s = jnp.where(ok, s, MASK_VALUE)
            m = jnp.max(s, axis=-1, keepdims=True)
            e = jnp.exp(s - m)
            l = jnp.sum(e, axis=-1, keepdims=True)
            o = jnp.dot(e.astype(v.dtype), v, preferred_element_type=F32) / l
            lse = jnp.broadcast_to(m + jnp.log(l), (DIL_TQ, LANES))
            dst = pl.ds(n * (DIL_TQ * d) + p, DIL_TQ, stride=d) if d > 1 else pl.ds(q0, DIL_TQ)
            o_scr[gi][dst, :] = o
            l_scr[gi][dst, :] = lse
        return carry

    lax.fori_loop(0, seq // DIL_TQ, step, 0, unroll=DIL_UNROLL)

    l0, l1, l2 = l_scr[0][...], l_scr[1][...], l_scr[2][...]
    m = jnp.maximum(jnp.maximum(l0, l1), l2)
    e0, e1, e2 = jnp.exp(l0 - m), jnp.exp(l1 - m), jnp.exp(l2 - m)
    num = e0 * o_scr[0][...] + e1 * o_scr[1][...] + e2 * o_scr[2][...]
    o_ref[...] = (num / (e0 + e1 + e2)).astype(o_ref.dtype)


def _dil_attention(v_plain, groups, seq):
    t = v_plain.shape[0]
    blk = lambda off: pl.BlockSpec((seq, LANES), lambda b, h: (b, off + h))
    qs = [qk for qk, _ in groups]
    vs = [v_plain if v is None else v for _, v in groups]
    v_off = [A_WIDTH // LANES if v is None else 0 for _, v in groups]
    in_specs = ([blk(0)] * B_GROUPS + [blk(B_HEADS)] * B_GROUPS + [blk(off) for off in v_off])
    return pl.pallas_call(
        functools.partial(_dil_kernel, seq=seq),
        grid=(t // seq, B_HEADS),
        in_specs=in_specs,
        out_specs=pl.BlockSpec((seq, LANES), lambda b, h: (b, h)),
        out_shape=jax.ShapeDtypeStruct((t, B_OUT), BF16),
        scratch_shapes=[pltpu.VMEM((seq, LANES), F32)] * (2 * B_GROUPS),
        compiler_params=_params("parallel", "parallel"),
        name="dilated_attention",
    )(*qs, *qs, *vs)


def _mix_kernel(oa_ref, ob_ref, ga_ref, gb_ref, x_ref, wpa_ref, wpb_ref, wo_ref, g_ref,
                xo_ref, hn_ref):
    ya = jnp.dot(oa_ref[...], wpa_ref[...], preferred_element_type=F32)
    yb = jnp.dot(ob_ref[...], wpb_ref[...], preferred_element_type=F32)
    merged = (ga_ref[...] * ya + gb_ref[...] * yb).astype(BF16)
    xo = x_ref[...] + jnp.dot(merged, wo_ref[...], preferred_element_type=F32)
    xo_ref[...] = xo
    hn_ref[...] = _rms(xo, g_ref[...]).astype(hn_ref.dtype)


def _mix_out(oa, ob, gates, x, w_pa, w_pb, w_out, norm_ffn, layer):
    t = x.shape[0]
    row = lambda width, col=0: pl.BlockSpec((MIX_TM, width), lambda i: (i, col))
    whole = lambda k, n: pl.BlockSpec((None, k, n), lambda i: (layer, 0, 0),
                                      pipeline_mode=pl.Buffered(1))
    return pl.pallas_call(
        _mix_kernel,
        grid=(t // MIX_TM,),
        in_specs=[row(A_WIDTH), row(B_OUT), row(D_MODEL, 0), row(D_MODEL, 1), row(D_MODEL),
                  whole(A_WIDTH, D_MODEL), whole(B_OUT, D_MODEL), whole(D_MODEL, D_MODEL),
                  pl.BlockSpec((None, 1, D_MODEL), lambda i: (layer, 0, 0))],
        out_specs=[row(D_MODEL), row(D_MODEL)],
        out_shape=[jax.ShapeDtypeStruct((t, D_MODEL), F32),
                   jax.ShapeDtypeStruct((t, D_MODEL), BF16)],
        compiler_params=_params("parallel"),
        name="mix_out",
    )(oa, ob, gates, gates, x, w_pa, w_pb, w_out, norm_ffn)


def _ffn_kernel(hn_ref, wg_ref, wu_ref, w2_ref, x_ref, g_ref, *rest, emit_x):
    outs, acc_ref = rest[:-1], rest[-1]
    k = pl.program_id(1)

    @pl.when(k == 0)
    def _():
        acc_ref[...] = x_ref[...]

    hn = hn_ref[...]
    width = FFN_TF // FFN_CHUNKS
    acts = []
    for c in range(FFN_CHUNKS):
        cols = slice(c * width, (c + 1) * width)
        hg = jnp.dot(hn, wg_ref[:, cols], preferred_element_type=F32)
        hu = jnp.dot(hn, wu_ref[:, cols], preferred_element_type=F32)
        acts.append((hg * _sigmoid(hg) * hu).astype(BF16))
    part = None
    for c in range(FFN_CHUNKS):
        rows = slice(c * width, (c + 1) * width)
        p = jnp.dot(acts[c], w2_ref[rows, :], preferred_element_type=F32)
        part = p if part is None else part + p
    acc_ref[...] += part

    @pl.when(k == pl.num_programs(1) - 1)
    def _():
        xo = acc_ref[...]
        if emit_x:
            outs[0][...] = xo
        outs[-1][...] = _rms(xo, g_ref[...]).astype(outs[-1].dtype)


def _ffn(hn, w_ffn_in, w_ffn_out, x, norm_all, layer, norm_layer, emit_x):
    t = x.shape[0]
    nk = D_FF // FFN_TF
    row = pl.BlockSpec((FFN_TM, D_MODEL), lambda i, k: (i, 0))
    if norm_all.ndim == 3:
        g_spec = pl.BlockSpec((None, 1, D_MODEL), lambda i, k: (norm_layer, 0, 0))
    else:
        g_spec = pl.BlockSpec((1, D_MODEL), lambda i, k: (0, 0))
    out_specs = [row]
    out_shape = [jax.ShapeDtypeStruct((t, D_MODEL), BF16 if emit_x else F32)]
    if emit_x:
        out_specs = [row, row]
        out_shape = [jax.ShapeDtypeStruct((t, D_MODEL), F32)] + out_shape
    return pl.pallas_call(
        functools.partial(_ffn_kernel, emit_x=emit_x),
        grid=(t // FFN_TM, nk),
        in_specs=[row,
                  pl.BlockSpec((None, D_MODEL, FFN_TF), lambda i, k: (layer, 0, k)),
                  pl.BlockSpec((None, D_MODEL, FFN_TF), lambda i, k: (layer, 0, nk + k)),
                  pl.BlockSpec((None, FFN_TF, D_MODEL), lambda i, k: (layer, k, 0)),
                  row, g_spec],
        out_specs=out_specs,
        out_shape=out_shape,
        scratch_shapes=[pltpu.VMEM((FFN_TM, D_MODEL), F32)],
        compiler_params=_params("parallel", "arbitrary"),
        name="ffn",
    )(hn, w_ffn_in, w_ffn_in, w_ffn_out, x, norm_all)


def _rope_lane_tables(seq, head_dim):
    rot = head_dim // ROPE_FRAC
    inv = ROPE_THETA ** (-(jnp.arange(0, rot, 2, dtype=F32) / rot))
    ang = jnp.arange(seq, dtype=F32)[:, None] * inv[None, :]
    cos, sin = jnp.cos(ang), jnp.sin(ang)
    rest = head_dim - rot
    c = jnp.concatenate([cos, cos, jnp.ones((seq, rest), F32)], axis=-1)
    s = jnp.concatenate([-sin, sin, jnp.zeros((seq, rest), F32)], axis=-1)
    reps = LANES // head_dim
    return jnp.tile(c, (1, reps)), jnp.tile(s, (1, reps))


def _trunk(x, seq, norm_mix, norm_ffn, w_in, gate_bias, diff_lambda, diff_subln,
           w_pa, w_pb, w_out, w_ffn_in, w_ffn_out, norm_final):
    tables = _rope_lane_tables(seq, A_HEAD_DIM) + _rope_lane_tables(seq, B_HEAD_DIM)
    xn = _rmsnorm(x, norm_mix, 0)
    for layer in range(DEPTH):
        a_qk, v_plain, groups = _qkv_proj(xn, w_in, layer, tables, seq)
        gates = _gate_proj(xn, w_in, gate_bias, layer, seq)
        oa = _diff_attention(a_qk, v_plain, diff_lambda, diff_subln, layer, seq)
        ob = _dil_attention(v_plain, groups, seq)
        x, hn = _mix_out(oa, ob, gates, x, w_pa, w_pb, w_out, norm_ffn, layer)
        if layer + 1 < DEPTH:
            x, xn = _ffn(hn, w_ffn_in, w_ffn_out, x, norm_mix, layer, layer + 1, True)
        else:
            (y,) = _ffn(hn, w_ffn_in, w_ffn_out, x, norm_final, layer, 0, False)
    return y


def kernel(x_prompt, x_sample, norm_mix, norm_ffn, w_in, gate_bias, diff_lambda, diff_subln,
           w_proj_a, w_proj_b, w_out, w_ffn_in, w_ffn_out, norm_final):
    bp, seq, d = x_prompt.shape
    bs = x_sample.shape[0]
    assert x_sample.shape[1:] == (seq, d) and d == D_MODEL
    assert seq % (B_PATTERNS[-1][1] * DIL_TQ) == 0 and seq >= DIL_WIN
    x = jnp.concatenate([x_prompt.reshape(bp * seq, d), x_sample.reshape(bs * seq, d)], axis=0)
    y = _trunk(
        x, seq,
        norm_mix.reshape(DEPTH, 1, D_MODEL), norm_ffn.reshape(DEPTH, 1, D_MODEL),
        w_in.astype(BF16), gate_bias.reshape(DEPTH, 1, GATE_WIDTH),
        diff_lambda, diff_subln.reshape(DEPTH, 2 * A_HEAD_DIM, 1),
        w_proj_a.astype(BF16), w_proj_b.astype(BF16), w_out.astype(BF16),
        w_ffn_in.astype(BF16), w_ffn_out.astype(BF16), norm_final.reshape(1, D_MODEL))
    return (y[:bp * seq].reshape(bp, seq, d), y[bp * seq:].reshape(bs, seq, d))
```

```python
import functools
import math

import jax
import jax.numpy as jnp
from jax import lax
from jax.experimental import pallas as pl
from jax.experimental.pallas import tpu as pltpu

D_MODEL = 2048
DEPTH = 4
A_HEADS = 8
A_HEAD_DIM = 64
A_WIDTH = A_HEADS * 2 * A_HEAD_DIM
B_PATTERNS = ((128, 1), (512, 4), (2048, 16))
B_GROUPS = len(B_PATTERNS)
B_HEADS = 4
B_HEAD_DIM = 128
B_WIDTH = B_GROUPS * B_HEADS * B_HEAD_DIM
B_OUT = B_HEADS * B_HEAD_DIM
QKV_WIDTH = 3 * A_WIDTH + 3 * B_WIDTH
GATE_WIDTH = 2 * D_MODEL
D_FF = 5632
ROPE_THETA = 500000.0
ROPE_FRAC = 4
NORM_EPS = 1e-6
MASK_VALUE = -1e30

LANES = 128
BF16_ROWS = 16
MXU_WIDTH = 256
VMEM_LIMIT = 60 * 1024 * 1024

PROJ_TN = 512
DIFF_TQ = 512
DIL_UNROLL = 4
DIL_TQ = 128
DIL_HALF = 64
DIL_WIN = 256
MIX_TM = 512
FFN_TM = 512
FFN_TF = 512
FFN_CHUNKS = 2
NORM_TM = 512

BF16 = jnp.bfloat16
F32 = jnp.float32


def _params(*sem):
    return pltpu.CompilerParams(dimension_semantics=sem, vmem_limit_bytes=VMEM_LIMIT)


def _sigmoid(x):
    return 0.5 * jnp.tanh(0.5 * x) + 0.5


def _rms(x, g):
    ms = jnp.mean(x * x, axis=-1, keepdims=True)
    return x * lax.rsqrt(ms + NORM_EPS) * g


def _stream_specs(parts, tm):
    specs, firsts, start = [], [], 0
    for arr in parts:
        n = arr.shape[0] // tm
        specs.append(pl.BlockSpec((tm, D_MODEL),
                                  lambda i, *_, start=start, n=n: (jnp.clip(i - start, 0, n - 1), 0)))
        firsts.append(start)
        start += n
    return specs, tuple(firsts)


def _stream_tile(refs, firsts, i):
    x = refs[0][...]
    for ref, first in zip(refs[1:], firsts[1:]):
        x = jnp.where(i >= first, ref[...], x)
    return x


def _rmsnorm_kernel(*refs, firsts):
    x_refs, (g_ref, x_ref, o_ref) = refs[:-3], refs[-3:]
    x = _stream_tile(x_refs, firsts, pl.program_id(0))
    x_ref[...] = x
    o_ref[...] = _rms(x, g_ref[...]).astype(o_ref.dtype)


def _rmsnorm(parts, g_all, layer):
    t = sum(p.shape[0] for p in parts)
    specs, firsts = _stream_specs(parts, NORM_TM)
    row = pl.BlockSpec((NORM_TM, D_MODEL), lambda i: (i, 0))
    return pl.pallas_call(
        functools.partial(_rmsnorm_kernel, firsts=firsts),
        grid=(t // NORM_TM,),
        in_specs=specs + [pl.BlockSpec((None, 1, D_MODEL), lambda i: (layer, 0, 0))],
        out_specs=[row, row],
        out_shape=[jax.ShapeDtypeStruct((t, D_MODEL), F32), jax.ShapeDtypeStruct((t, D_MODEL), BF16)],
        compiler_params=_params("parallel"),
        name="rmsnorm_in",
    )(*parts, g_all)


def _rope_chunk(x, c, s, head_dim):
    half = head_dim // ROPE_FRAC // 2
    lane = lax.broadcasted_iota(jnp.int32, x.shape, 1) % head_dim
    up = pltpu.roll(x, LANES - half, 1)
    down = pltpu.roll(x, half, 1)
    return x * c + jnp.where(lane < half, up, down) * s


def _proj_kernel(x_ref, w_ref, *rest, seq, head_dim, d, tn, scaled_tiles):
    if head_dim:
        c_ref, s_ref, o_ref, *scr = rest
    else:
        o_ref, *scr = rest
    j = pl.program_id(1)
    n_sub = seq // d
    x = x_ref[...]
    for m in range(tn // MXU_WIDTH):
        acc = jnp.dot(x, w_ref[:, m * MXU_WIDTH:(m + 1) * MXU_WIDTH], preferred_element_type=F32)
        for c in range(MXU_WIDTH // LANES):
            val = acc[:, c * LANES:(c + 1) * LANES]
            chunk = m * (MXU_WIDTH // LANES) + c
            sl = slice(chunk * LANES, (chunk + 1) * LANES)
            if head_dim:
                val = _rope_chunk(val, c_ref[...], s_ref[...], head_dim)
            if scaled_tiles:
                val = val * jnp.where(j < scaled_tiles, A_HEAD_DIM ** -0.5, 1.0).astype(F32)
            if d == 1:
                o_ref[:, sl] = val.astype(o_ref.dtype)
            else:
                scr[chunk][...] = val
                for p in range(d):
                    o_ref[p * n_sub:(p + 1) * n_sub, sl] = (
                        scr[chunk][pl.ds(p, n_sub, stride=d), :].astype(o_ref.dtype))


def _proj_class(xn, w_in, layer, seq, name, tiles, tn, tables=None, head_dim=0, d=1, scaled_tiles=0):
    t = xn.shape[0]

    def w_map(i, j):
        tile = tiles[0]
        for idx in range(1, len(tiles)):
            tile = jnp.where(j >= idx, tiles[idx], tile)
        return (layer, 0, tile)

    in_specs = [pl.BlockSpec((seq, D_MODEL), lambda i, j: (i, 0)),
                pl.BlockSpec((None, D_MODEL, tn), w_map)]
    args = [xn, w_in]
    if head_dim:
        in_specs += [pl.BlockSpec((seq, LANES), lambda i, j: (0, 0))] * 2
        args += list(tables)
    return pl.pallas_call(
        functools.partial(_proj_kernel, seq=seq, head_dim=head_dim, d=d, tn=tn,
                          scaled_tiles=scaled_tiles),
        grid=(t // seq, len(tiles)),
        in_specs=in_specs,
        out_specs=pl.BlockSpec((seq, tn), lambda i, j: (i, j)),
        out_shape=jax.ShapeDtypeStruct((t, len(tiles) * tn), BF16),
        scratch_shapes=[pltpu.VMEM((seq, LANES), F32)] * (tn // LANES if d > 1 else 0),
        compiler_params=_params("parallel", "arbitrary"),
        name=name,
    )(*args)


def _qkv_proj(xn, w_in, layer, tables, seq):
    tab_a, tab_b = tables[:2], tables[2:]
    wide = 2 * PROJ_TN
    n_a = 3 * A_WIDTH // PROJ_TN
    n_b = B_WIDTH // PROJ_TN
    assert n_b == B_GROUPS and A_WIDTH == wide
    proj = functools.partial(_proj_class, xn, w_in, layer, seq)
    a_qk = proj("proj_a_qk", (0, 1), wide, tables=tab_a, head_dim=A_HEAD_DIM, scaled_tiles=1)
    v_plain = proj("proj_v_plain", (n_a - 2, n_a - 1, n_a + 2 * n_b), PROJ_TN)
    groups = []
    for gi, (_, d) in enumerate(B_PATTERNS):
        qk = proj(f"proj_b_qk_d{d}", (n_a + gi, n_a + n_b + gi), PROJ_TN,
                  tables=tab_b, head_dim=B_HEAD_DIM, d=d)
        v = None if d == 1 else proj(f"proj_b_v_d{d}", (n_a + 2 * n_b + gi,), PROJ_TN, d=d)
        groups.append((qk, v))
    return a_qk, v_plain, groups


def _gate_kernel(x_ref, w_ref, b_ref, o_ref):
    x = x_ref[...]
    for m in range(PROJ_TN // MXU_WIDTH):
        cols = slice(m * MXU_WIDTH, (m + 1) * MXU_WIDTH)
        z = jnp.dot(x, w_ref[:, cols], preferred_element_type=F32) + b_ref[:, cols]
        o_ref[:, cols] = _sigmoid(z)


def _gate_proj(xn, w_in, bias, layer, seq):
    t = xn.shape[0]
    col0 = QKV_WIDTH // PROJ_TN
    return pl.pallas_call(
        _gate_kernel,
        grid=(t // seq, GATE_WIDTH // PROJ_TN),
        in_specs=[pl.BlockSpec((seq, D_MODEL), lambda i, j: (i, 0)),
                  pl.BlockSpec((None, D_MODEL, PROJ_TN), lambda i, j: (layer, 0, col0 + j)),
                  pl.BlockSpec((None, 1, PROJ_TN), lambda i, j: (layer, 0, j))],
        out_specs=pl.BlockSpec((seq, PROJ_TN), lambda i, j: (i, j)),
        out_shape=jax.ShapeDtypeStruct((t, GATE_WIDTH), F32),
        compiler_params=_params("parallel", "arbitrary"),
        name="gate_proj",
    )(xn, w_in, bias)


def _nt_dot(a, b):
    return lax.dot_general(a, b, (((1,), (1,)), ((), ())), preferred_element_type=F32)


def _diff_kernel(q_ref, k_ref, v_ref, lam_ref, g_ref, o_ref, vt_ref, *, seq, lam_init):
    lf = lam_ref[...]
    lam = (jnp.exp(jnp.sum(lf[0:1] * lf[1:2], axis=-1, keepdims=True))
           - jnp.exp(jnp.sum(lf[2:3] * lf[3:4], axis=-1, keepdims=True)) + lam_init)
    hd = 2 * A_HEAD_DIM
    k = k_ref[...]
    vt_ref[0:hd, :] = v_ref[...].astype(F32).T.astype(BF16)
    vt_ref[hd:, :] = jnp.ones((vt_ref.shape[0] - hd, seq), BF16)
    vt = vt_ref[...]
    g = g_ref[...] * (1.0 - lam_init)
    lane = lax.broadcasted_iota(jnp.int32, (DIFF_TQ, LANES), 1)
    first = lane < A_HEAD_DIM

    def scores(t):
        q = q_ref[t * DIFF_TQ:(t + 1) * DIFF_TQ, :]
        zero = jnp.zeros_like(q)
        return (_nt_dot(k, jnp.where(first, q, zero)),
                _nt_dot(k, jnp.where(first, zero, q)))

    nblk = seq // DIFF_TQ
    s_next = scores(0)
    for t in range(nblk):
        s1, s2 = s_next
        if t + 1 < nblk:
            s_next = scores(t + 1)
        e1 = jnp.exp(s1 - jnp.max(s1, axis=0, keepdims=True)).astype(BF16)
        e2 = jnp.exp(s2 - jnp.max(s2, axis=0, keepdims=True)).astype(BF16)
        u1 = jnp.dot(vt, e1, preferred_element_type=F32)
        u2 = jnp.dot(vt, e2, preferred_element_type=F32)
        o = u1[:hd] * (1.0 / u1[hd:hd + 1]) - u2[:hd] * (lam / u2[hd:hd + 1])
        ms = jnp.mean(o * o, axis=0, keepdims=True)
        o_ref[t * DIFF_TQ:(t + 1) * DIFF_TQ, :] = (
            (o * lax.rsqrt(ms + NORM_EPS) * g).T.astype(o_ref.dtype))


def _diff_attention(a_qk, v_plain, lam_all, subln_all, layer, seq):
    t = a_qk.shape[0]
    kb = A_WIDTH // LANES
    hd = 2 * A_HEAD_DIM
    blk = lambda off: pl.BlockSpec((seq, LANES), lambda b, h: (b, off + h))
    return pl.pallas_call(
        functools.partial(_diff_kernel, seq=seq, lam_init=0.8 - 0.6 * math.exp(-0.3 * layer)),
        grid=(t // seq, A_HEADS),
        in_specs=[blk(0), blk(kb), blk(0),
                  pl.BlockSpec((None, 4, A_HEAD_DIM), lambda b, h: (layer, 0, 0)),
                  pl.BlockSpec((None, hd, 1), lambda b, h: (layer, 0, 0))],
        out_specs=pl.BlockSpec((seq, LANES), lambda b, h: (b, h)),
        out_shape=jax.ShapeDtypeStruct((t, A_WIDTH), BF16),
        scratch_shapes=[pltpu.VMEM((hd + BF16_ROWS, seq), BF16)],
        compiler_params=_params("parallel", "parallel"),
        name="diff_attention",
    )(a_qk, a_qk, v_plain, lam_all, subln_all)


def _dil_kernel(*refs, seq):
    q_refs, k_refs, v_refs = refs[0:3], refs[3:6], refs[6:9]
    o_ref = refs[9]
    o_scr, l_scr = refs[10:13], refs[13:16]
    scale = B_HEAD_DIM ** -0.5
    row = lax.broadcasted_iota(jnp.int32, (DIL_TQ, DIL_WIN), 0)
    col = lax.broadcasted_iota(jnp.int32, (DIL_TQ, DIL_WIN), 1)
    rel = col - row

    def step(t, carry):
        q0 = pl.multiple_of(t * DIL_TQ, DIL_TQ)
        ws = pl.multiple_of(jnp.clip(q0 - DIL_HALF, 0, seq - DIL_WIN), DIL_HALF)
        band = jnp.abs(rel + (ws - q0)) <= DIL_HALF
        for gi, (window, d) in enumerate(B_PATTERNS):
            assert window // (2 * d) == DIL_HALF
            n_sub = seq // d
            nblk = n_sub // DIL_TQ
            p = t // nblk
            n = t % nblk
            lo = p * n_sub - ws
            q = q_refs[gi][pl.ds(q0, DIL_TQ), :]
            k = k_refs[gi][pl.ds(ws, DIL_WIN), :]
            v = v_refs[gi][pl.ds(ws, DIL_WIN), :]
            s = _nt_dot(q, k) * scale
            ok = band & (col >= lo) & (col < lo + n_sub) if d > 1 else band
            s = jnp.where(ok, s, MASK_VALUE)
            m = jnp.max(s, axis=-1, keepdims=True)
            e = jnp.exp(s - m)
            l = jnp.sum(e, axis=-1, keepdims=True)
            o = jnp.dot(e.astype(v.dtype), v, preferred_element_type=F32) / l
            lse = jnp.broadcast_to(m + jnp.log(l), (DIL_TQ, LANES))
            dst = pl.ds(n * (DIL_TQ * d) + p, DIL_TQ, stride=d) if d > 1 else pl.ds(q0, DIL_TQ)
            o_scr[gi][dst, :] = o
            l_scr[gi][dst, :] = lse
        return carry

    lax.fori_loop(0, seq // DIL_TQ, step, 0, unroll=DIL_UNROLL)

    l0, l1, l2 = l_scr[0][...], l_scr[1][...], l_scr[2][...]
    m = jnp.maximum(jnp.maximum(l0, l1), l2)
    e0, e1, e2 = jnp.exp(l0 - m), jnp.exp(l1 - m), jnp.exp(l2 - m)
    num = e0 * o_scr[0][...] + e1 * o_scr[1][...] + e2 * o_scr[2][...]
    o_ref[...] = (num / (e0 + e1 + e2)).astype(o_ref.dtype)


def _dil_attention(v_plain, groups, seq):
    t = v_plain.shape[0]
    blk = lambda off: pl.BlockSpec((seq, LANES), lambda b, h: (b, off + h))
    qs = [qk for qk, _ in groups]
    vs = [v_plain if v is None else v for _, v in groups]
    v_off = [A_WIDTH // LANES if v is None else 0 for _, v in groups]
    in_specs = ([blk(0)] * B_GROUPS + [blk(B_HEADS)] * B_GROUPS + [blk(off) for off in v_off])
    return pl.pallas_call(
        functools.partial(_dil_kernel, seq=seq),
        grid=(t // seq, B_HEADS),
        in_specs=in_specs,
        out_specs=pl.BlockSpec((seq, LANES), lambda b, h: (b, h)),
        out_shape=jax.ShapeDtypeStruct((t, B_OUT), BF16),
        scratch_shapes=[pltpu.VMEM((seq, LANES), F32)] * (2 * B_GROUPS),
        compiler_params=_params("parallel", "parallel"),
        name="dilated_attention",
    )(*qs, *qs, *vs)


def _mix_kernel(oa_ref, ob_ref, ga_ref, gb_ref, x_ref, wpa_ref, wpb_ref, wo_ref, g_ref,
                xo_ref, hn_ref):
    ya = jnp.dot(oa_ref[...], wpa_ref[...], preferred_element_type=F32)
    yb = jnp.dot(ob_ref[...], wpb_ref[...], preferred_element_type=F32)
    merged = (ga_ref[...] * ya + gb_ref[...] * yb).astype(BF16)
    xo = x_ref[...] + jnp.dot(merged, wo_ref[...], preferred_element_type=F32)
    xo_ref[...] = xo
    hn_ref[...] = _rms(xo, g_ref[...]).astype(hn_ref.dtype)


def _mix_out(oa, ob, gates, x, w_pa, w_pb, w_out, norm_ffn, layer):
    t = x.shape[0]
    row = lambda width, col=0: pl.BlockSpec((MIX_TM, width), lambda i: (i, col))
    whole = lambda k, n: pl.BlockSpec((None, k, n), lambda i: (layer, 0, 0),
                                      pipeline_mode=pl.Buffered(1))
    return pl.pallas_call(
        _mix_kernel,
        grid=(t // MIX_TM,),
        in_specs=[row(A_WIDTH), row(B_OUT), row(D_MODEL, 0), row(D_MODEL, 1), row(D_MODEL),
                  whole(A_WIDTH, D_MODEL), whole(B_OUT, D_MODEL), whole(D_MODEL, D_MODEL),
                  pl.BlockSpec((None, 1, D_MODEL), lambda i: (layer, 0, 0))],
        out_specs=[row(D_MODEL), row(D_MODEL)],
        out_shape=[jax.ShapeDtypeStruct((t, D_MODEL), F32),
                   jax.ShapeDtypeStruct((t, D_MODEL), BF16)],
        compiler_params=_params("parallel"),
        name="mix_out",
    )(oa, ob, gates, gates, x, w_pa, w_pb, w_out, norm_ffn)


def _ffn_kernel(hn_ref, wg_ref, wu_ref, w2_ref, x_ref, g_ref, *rest, emit_x):
    outs, acc_ref = rest[:-1], rest[-1]
    k = pl.program_id(1)

    @pl.when(k == 0)
    def _():
        acc_ref[...] = x_ref[...]

    hn = hn_ref[...]
    width = FFN_TF // FFN_CHUNKS
    acts = []
    for c in range(FFN_CHUNKS):
        cols = slice(c * width, (c + 1) * width)
        hg = jnp.dot(hn, wg_ref[:, cols], preferred_element_type=F32)
        hu = jnp.dot(hn, wu_ref[:, cols], preferred_element_type=F32)
        acts.append((hg * _sigmoid(hg) * hu).astype(BF16))
    part = None
    for c in range(FFN_CHUNKS):
        rows = slice(c * width, (c + 1) * width)
        p = jnp.dot(acts[c], w2_ref[rows, :], preferred_element_type=F32)
        part = p if part is None else part + p
    acc_ref[...] += part

    @pl.when(k == pl.num_programs(1) - 1)
    def _():
        xo = acc_ref[...]
        if emit_x:
            outs[0][...] = xo
        outs[-1][...] = _rms(xo, g_ref[...]).astype(outs[-1].dtype)


def _ffn(hn, w_ffn_in, w_ffn_out, x, norm_all, layer, norm_layer, emit_x, tile0=0, n_tiles=None):
    if n_tiles is None:
        n_tiles = x.shape[0] // FFN_TM
    t = n_tiles * FFN_TM
    nk = D_FF // FFN_TF
    row_in = pl.BlockSpec((FFN_TM, D_MODEL), lambda i, k: (tile0 + i, 0))
    row_out = pl.BlockSpec((FFN_TM, D_MODEL), lambda i, k: (i, 0))
    if norm_all.ndim == 3:
        g_spec = pl.BlockSpec((None, 1, D_MODEL), lambda i, k: (norm_layer, 0, 0))
    else:
        g_spec = pl.BlockSpec((1, D_MODEL), lambda i, k: (0, 0))
    out_specs = [row_out]
    out_shape = [jax.ShapeDtypeStruct((t, D_MODEL), BF16 if emit_x else F32)]
    if emit_x:
        out_specs = [row_out, row_out]
        out_shape = [jax.ShapeDtypeStruct((t, D_MODEL), F32)] + out_shape
    return pl.pallas_call(
        functools.partial(_ffn_kernel, emit_x=emit_x),
        grid=(n_tiles, nk),
        in_specs=[row_in,
                  pl.BlockSpec((None, D_MODEL, FFN_TF), lambda i, k: (layer, 0, k)),
                  pl.BlockSpec((None, D_MODEL, FFN_TF), lambda i, k: (layer, 0, nk + k)),
                  pl.BlockSpec((None, FFN_TF, D_MODEL), lambda i, k: (layer, k, 0)),
                  row_in, g_spec],
        out_specs=out_specs,
        out_shape=out_shape,
        scratch_shapes=[pltpu.VMEM((FFN_TM, D_MODEL), F32)],
        compiler_params=_params("parallel", "arbitrary"),
        name="ffn",
    )(hn, w_ffn_in, w_ffn_in, w_ffn_out, x, norm_all)


def _rope_lane_tables(seq, head_dim):
    rot = head_dim // ROPE_FRAC
    inv = ROPE_THETA ** (-(jnp.arange(0, rot, 2, dtype=F32) / rot))
    ang = jnp.arange(seq, dtype=F32)[:, None] * inv[None, :]
    cos, sin = jnp.cos(ang), jnp.sin(ang)
    rest = head_dim - rot
    c = jnp.concatenate([cos, cos, jnp.ones((seq, rest), F32)], axis=-1)
    s = jnp.concatenate([-sin, sin, jnp.zeros((seq, rest), F32)], axis=-1)
    reps = LANES // head_dim
    return jnp.tile(c, (1, reps)), jnp.tile(s, (1, reps))


def _trunk(x_parts, seq, norm_mix, norm_ffn, w_in, gate_bias, diff_lambda, diff_subln,
           w_pa, w_pb, w_out, w_ffn_in, w_ffn_out, norm_final):
    tables = _rope_lane_tables(seq, A_HEAD_DIM) + _rope_lane_tables(seq, B_HEAD_DIM)
    x, xn = _rmsnorm(x_parts, norm_mix, 0)
    for layer in range(DEPTH):
        a_qk, v_plain, groups = _qkv_proj(xn, w_in, layer, tables, seq)
        gates = _gate_proj(xn, w_in, gate_bias, layer, seq)
        oa = _diff_attention(a_qk, v_plain, diff_lambda, diff_subln, layer, seq)
        ob = _dil_attention(v_plain, groups, seq)
        x, hn = _mix_out(oa, ob, gates, x, w_pa, w_pb, w_out, norm_ffn, layer)
        if layer + 1 < DEPTH:
            x, xn = _ffn(hn, w_ffn_in, w_ffn_out, x, norm_mix, layer, layer + 1, True)
    outs, tile0 = [], 0
    for part in x_parts:
        n_tiles = part.shape[0] // FFN_TM
        (y,) = _ffn(hn, w_ffn_in, w_ffn_out, x, norm_final, DEPTH - 1, 0, False, tile0, n_tiles)
        outs.append(y)
        tile0 += n_tiles
    return outs


def kernel(x_prompt, x_sample, norm_mix, norm_ffn, w_in, gate_bias, diff_lambda, diff_subln,
           w_proj_a, w_proj_b, w_out, w_ffn_in, w_ffn_out, norm_final):
    bp, seq, d = x_prompt.shape
    bs = x_sample.shape[0]
    assert x_sample.shape[1:] == (seq, d) and d == D_MODEL
    assert seq % (B_PATTERNS[-1][1] * DIL_TQ) == 0 and seq >= DIL_WIN
    y_prompt, y_sample = _trunk(
        [x_prompt.reshape(bp * seq, d), x_sample.reshape(bs * seq, d)], seq,
        norm_mix.reshape(DEPTH, 1, D_MODEL), norm_ffn.reshape(DEPTH, 1, D_MODEL),
        w_in.astype(BF16), gate_bias.reshape(DEPTH, 1, GATE_WIDTH),
        diff_lambda, diff_subln.reshape(DEPTH, 2 * A_HEAD_DIM, 1),
        w_proj_a.astype(BF16), w_proj_b.astype(BF16), w_out.astype(BF16),
        w_ffn_in.astype(BF16), w_ffn_out.astype(BF16), norm_final.reshape(1, D_MODEL))
    return (y_prompt.reshape(bp, seq, d), y_sample.reshape(bs, seq, d))
```

```python
import functools
import math

import jax
import jax.numpy as jnp
from jax import lax
from jax.experimental import pallas as pl
from jax.experimental.pallas import tpu as pltpu

D_MODEL = 2048
DEPTH = 4
A_HEADS = 8
A_HEAD_DIM = 64
A_WIDTH = A_HEADS * 2 * A_HEAD_DIM
B_PATTERNS = ((128, 1), (512, 4), (2048, 16))
B_GROUPS = len(B_PATTERNS)
B_HEADS = 4
B_HEAD_DIM = 128
B_WIDTH = B_GROUPS * B_HEADS * B_HEAD_DIM
B_OUT = B_HEADS * B_HEAD_DIM
QKV_WIDTH = 3 * A_WIDTH + 3 * B_WIDTH
GATE_WIDTH = 2 * D_MODEL
D_FF = 5632
ROPE_THETA = 500000.0
ROPE_FRAC = 4
NORM_EPS = 1e-6
MASK_VALUE = -1e30

LANES = 128
BF16_ROWS = 16
MXU_WIDTH = 256
VMEM_LIMIT = 60 * 1024 * 1024

PROJ_TN = 512
DIFF_TQ = 512
DIL_UNROLL = 4
DIL_TQ = 128
DIL_HALF = 64
DIL_WIN = 256
MAX_ROW_STRIDE = 4
MIX_TM = 512
FFN_TM = 512
FFN_TF = 512
FFN_CHUNKS = 2
NORM_TM = 512

BF16 = jnp.bfloat16
F32 = jnp.float32


def _params(*sem):
    return pltpu.CompilerParams(dimension_semantics=sem, vmem_limit_bytes=VMEM_LIMIT)


def _sigmoid(x):
    return 0.5 * jnp.tanh(0.5 * x) + 0.5


def _rms(x, g):
    ms = jnp.mean(x * x, axis=-1, keepdims=True)
    return x * lax.rsqrt(ms + NORM_EPS) * g


def _stream_specs(parts, tm):
    specs, firsts, start = [], [], 0
    for arr in parts:
        n = arr.shape[0] // tm
        specs.append(pl.BlockSpec((tm, D_MODEL),
                                  lambda i, *_, start=start, n=n: (jnp.clip(i - start, 0, n - 1), 0)))
        firsts.append(start)
        start += n
    return specs, tuple(firsts)


def _stream_tile(refs, firsts, i):
    x = refs[0][...]
    for ref, first in zip(refs[1:], firsts[1:]):
        x = jnp.where(i >= first, ref[...], x)
    return x


def _rmsnorm_kernel(*refs, firsts):
    x_refs, (g_ref, x_ref, o_ref) = refs[:-3], refs[-3:]
    x = _stream_tile(x_refs, firsts, pl.program_id(0))
    x_ref[...] = x
    o_ref[...] = _rms(x, g_ref[...]).astype(o_ref.dtype)


def _rmsnorm(parts, g_all, layer):
    t = sum(p.shape[0] for p in parts)
    specs, firsts = _stream_specs(parts, NORM_TM)
    row = pl.BlockSpec((NORM_TM, D_MODEL), lambda i: (i, 0))
    return pl.pallas_call(
        functools.partial(_rmsnorm_kernel, firsts=firsts),
        grid=(t // NORM_TM,),
        in_specs=specs + [pl.BlockSpec((None, 1, D_MODEL), lambda i: (layer, 0, 0))],
        out_specs=[row, row],
        out_shape=[jax.ShapeDtypeStruct((t, D_MODEL), F32), jax.ShapeDtypeStruct((t, D_MODEL), BF16)],
        compiler_params=_params("parallel"),
        name="rmsnorm_in",
    )(*parts, g_all)


def _stride_stages(d):
    if d <= MAX_ROW_STRIDE:
        return d, 1
    assert d % MAX_ROW_STRIDE == 0 and d // MAX_ROW_STRIDE <= MAX_ROW_STRIDE
    return MAX_ROW_STRIDE, d // MAX_ROW_STRIDE


def _block_class(blk, d):
    s1, s2 = _stride_stages(d)
    return blk if s2 == 1 else s1 * (blk % s2) + blk // s2


def _proj_kernel(x_ref, w_ref, *rest, seq, rope, d, tn, scaled_tiles):
    if rope:
        c_ref, s_ref, o_ref, *scr = rest
    else:
        o_ref, *scr = rest
    j = pl.program_id(1)
    n_sub = seq // d
    x = x_ref[...]
    assert MXU_WIDTH == 2 * LANES
    for m in range(tn // MXU_WIDTH):
        acc = jnp.dot(x, w_ref[:, m * MXU_WIDTH:(m + 1) * MXU_WIDTH], preferred_element_type=F32)
        halves = [acc[:, :LANES], acc[:, LANES:]]
        if rope == "paired":
            x1, x2 = halves
            cos, sin = c_ref[...], s_ref[...]
            halves = [x1 * cos - x2 * sin, x2 * cos + x1 * sin]
        elif rope == "split":
            halves = [h * c_ref[...] + pltpu.roll(h, LANES // 2, 1) * s_ref[...] for h in halves]
        for c, val in enumerate(halves):
            chunk = m * (MXU_WIDTH // LANES) + c
            sl = slice(chunk * LANES, (chunk + 1) * LANES)
            if scaled_tiles:
                val = val * jnp.where(j < scaled_tiles, A_HEAD_DIM ** -0.5, 1.0).astype(F32)
            if d == 1:
                o_ref[:, sl] = val.astype(o_ref.dtype)
                continue
            buf = scr[chunk]
            buf[...] = val
            s1, s2 = _stride_stages(d)
            if s2 > 1:
                tmp, n1 = scr[tn // LANES + chunk], seq // s1
                for p1 in range(s1):
                    tmp[p1 * n1:(p1 + 1) * n1, :] = buf[pl.ds(p1, n1, stride=s1), :]
                runs = [(tmp, p1 * n1, s2) for p1 in range(s1)]
            else:
                runs = [(buf, 0, s1)]
            blk = 0
            for src, base, stride in runs:
                for p in range(stride):
                    o_ref[blk * n_sub:(blk + 1) * n_sub, sl] = (
                        src[pl.ds(base + p, n_sub, stride=stride), :].astype(o_ref.dtype))
                    blk += 1


def _proj_class(xn, w_in, layer, seq, name, tiles, tn, rope=None, tables=None, d=1, scaled_tiles=0):
    t = xn.shape[0]

    def w_map(i, j):
        tile = tiles[0]
        for idx in range(1, len(tiles)):
            tile = jnp.where(j >= idx, tiles[idx], tile)
        return (layer, 0, tile)

    in_specs = [pl.BlockSpec((seq, D_MODEL), lambda i, j: (i, 0)),
                pl.BlockSpec((None, D_MODEL, tn), w_map)]
    args = [xn, w_in]
    if rope:
        in_specs += [pl.BlockSpec((seq, LANES), lambda i, j: (0, 0))] * 2
        args += list(tables)
    return pl.pallas_call(
        functools.partial(_proj_kernel, seq=seq, rope=rope, d=d, tn=tn, scaled_tiles=scaled_tiles),
        grid=(t // seq, len(tiles)),
        in_specs=in_specs,
        out_specs=pl.BlockSpec((seq, tn), lambda i, j: (i, j)),
        out_shape=jax.ShapeDtypeStruct((t, len(tiles) * tn), BF16),
        scratch_shapes=([pltpu.VMEM((seq, LANES), F32)]
                        * (tn // LANES * (0 if d == 1 else 1 if _stride_stages(d)[1] == 1 else 2))),
        compiler_params=_params("parallel", "arbitrary"),
        name=name,
    )(*args)


def _qkv_proj(xn, w_in, layer, tables, seq):
    tab_a, tab_b = tables[:2], tables[2:]
    wide = 2 * PROJ_TN
    n_a = 3 * A_WIDTH // PROJ_TN
    n_b = B_WIDTH // PROJ_TN
    assert n_b == B_GROUPS and A_WIDTH == wide
    proj = functools.partial(_proj_class, xn, w_in, layer, seq)
    a_qk = proj("proj_a_qk", (0, 1), wide, rope="paired", tables=tab_a, scaled_tiles=1)
    v_plain = proj("proj_v_plain", (n_a - 2, n_a - 1, n_a + 2 * n_b), PROJ_TN)
    groups = []
    for gi, (_, d) in enumerate(B_PATTERNS):
        qk = proj(f"proj_b_qk_d{d}", (n_a + gi, n_a + n_b + gi), PROJ_TN,
                  rope="split", tables=tab_b, d=d)
        v = None if d == 1 else proj(f"proj_b_v_d{d}", (n_a + 2 * n_b + gi,), PROJ_TN, d=d)
        groups.append((qk, v))
    return a_qk, v_plain, groups


def _pair_columns(w, head_dim):
    heads = MXU_WIDTH // head_dim
    half = head_dim // ROPE_FRAC // 2
    keep = (head_dim - 2 * half) // 2
    lead = w.shape[:-1]
    w = w.reshape(*lead, -1, heads, head_dim)
    flat = lambda piece: piece.reshape(*lead, piece.shape[-3], -1)
    x1, x2 = w[..., :half], w[..., half:2 * half]
    r1, r2 = w[..., 2 * half:2 * half + keep], w[..., 2 * half + keep:]
    out = jnp.concatenate([flat(x1), flat(r1), flat(x2), flat(r2)], axis=-1)
    return out.reshape(*lead, -1)


def _head_lane_mask(lane, head, head_dim):
    heads = MXU_WIDTH // head_dim
    half = head_dim // ROPE_FRAC // 2
    keep = (head_dim - 2 * half) // 2
    rot0 = head * half
    keep0 = heads * half + head * keep
    return ((lane >= rot0) & (lane < rot0 + half)) | ((lane >= keep0) & (lane < keep0 + keep))


def _paired_rope_tables(seq, head_dim):
    heads = MXU_WIDTH // head_dim
    rot = head_dim // ROPE_FRAC
    inv = ROPE_THETA ** (-(jnp.arange(0, rot, 2, dtype=F32) / rot))
    ang = jnp.arange(seq, dtype=F32)[:, None] * inv[None, :]
    rest = LANES - heads * (rot // 2)
    cos = jnp.concatenate([jnp.tile(jnp.cos(ang), (1, heads)), jnp.ones((seq, rest), F32)], axis=-1)
    sin = jnp.concatenate([jnp.tile(jnp.sin(ang), (1, heads)), jnp.zeros((seq, rest), F32)], axis=-1)
    return cos, sin


def _split_columns(w, head_dim):
    assert head_dim == LANES
    half = head_dim // ROPE_FRAC // 2
    keep = LANES // 2 - half
    lead = w.shape[:-1]
    w = w.reshape(*lead, -1, head_dim)
    x1, x2 = w[..., :half], w[..., half:2 * half]
    r1, r2 = w[..., 2 * half:2 * half + keep], w[..., 2 * half + keep:]
    return jnp.concatenate([x1, r1, x2, r2], axis=-1).reshape(*lead, -1)


def _split_rope_tables(seq, head_dim):
    rot = head_dim // ROPE_FRAC
    inv = ROPE_THETA ** (-(jnp.arange(0, rot, 2, dtype=F32) / rot))
    ang = jnp.arange(seq, dtype=F32)[:, None] * inv[None, :]
    cos, sin = jnp.cos(ang), jnp.sin(ang)
    rest = LANES // 2 - rot // 2
    one, zero = jnp.ones((seq, rest), F32), jnp.zeros((seq, rest), F32)
    return (jnp.concatenate([cos, one, cos, one], axis=-1),
            jnp.concatenate([-sin, zero, sin, zero], axis=-1))


def _prepare_w_in(w_in):
    a_qk, b0 = 2 * A_WIDTH, 3 * A_WIDTH
    return jnp.concatenate([
        _pair_columns(w_in[..., :a_qk], A_HEAD_DIM),
        w_in[..., a_qk:b0],
        _split_columns(w_in[..., b0:b0 + 2 * B_WIDTH], B_HEAD_DIM),
        w_in[..., b0 + 2 * B_WIDTH:],
    ], axis=-1).astype(BF16)


def _gate_kernel(x_ref, w_ref, b_ref, o_ref):
    x = x_ref[...]
    for m in range(PROJ_TN // MXU_WIDTH):
        cols = slice(m * MXU_WIDTH, (m + 1) * MXU_WIDTH)
        z = jnp.dot(x, w_ref[:, cols], preferred_element_type=F32) + b_ref[:, cols]
        o_ref[:, cols] = _sigmoid(z)


def _gate_proj(xn, w_in, bias, layer, seq):
    t = xn.shape[0]
    col0 = QKV_WIDTH // PROJ_TN
    return pl.pallas_call(
        _gate_kernel,
        grid=(t // seq, GATE_WIDTH // PROJ_TN),
        in_specs=[pl.BlockSpec((seq, D_MODEL), lambda i, j: (i, 0)),
                  pl.BlockSpec((None, D_MODEL, PROJ_TN), lambda i, j: (layer, 0, col0 + j)),
                  pl.BlockSpec((None, 1, PROJ_TN), lambda i, j: (layer, 0, j))],
        out_specs=pl.BlockSpec((seq, PROJ_TN), lambda i, j: (i, j)),
        out_shape=jax.ShapeDtypeStruct((t, GATE_WIDTH), F32),
        compiler_params=_params("parallel", "arbitrary"),
        name="gate_proj",
    )(xn, w_in, bias)


def _nt_dot(a, b):
    return lax.dot_general(a, b, (((1,), (1,)), ((), ())), preferred_element_type=F32)


def _diff_kernel(q_ref, k_ref, v_ref, lam_ref, g_ref, o_ref, vt_ref, *, seq, lam_init):
    lf = lam_ref[...]
    lam = (jnp.exp(jnp.sum(lf[0:1] * lf[1:2], axis=-1, keepdims=True))
           - jnp.exp(jnp.sum(lf[2:3] * lf[3:4], axis=-1, keepdims=True)) + lam_init)
    hd = 2 * A_HEAD_DIM
    k = k_ref[...]
    vt_ref[0:hd, :] = v_ref[...].astype(F32).T.astype(BF16)
    vt_ref[hd:, :] = jnp.ones((vt_ref.shape[0] - hd, seq), BF16)
    vt = vt_ref[...]
    g = g_ref[...] * (1.0 - lam_init)
    lane = lax.broadcasted_iota(jnp.int32, (DIFF_TQ, MXU_WIDTH), 1) % LANES
    first_map = 2 * (pl.program_id(1) % 2)
    in_map1 = _head_lane_mask(lane, first_map, A_HEAD_DIM)
    in_map2 = _head_lane_mask(lane, first_map + 1, A_HEAD_DIM)

    def scores(t):
        q = q_ref[t * DIFF_TQ:(t + 1) * DIFF_TQ, :]
        zero = jnp.zeros_like(q)
        return (_nt_dot(k, jnp.where(in_map1, q, zero)),
                _nt_dot(k, jnp.where(in_map2, q, zero)))

    nblk = seq // DIFF_TQ
    s_next = scores(0)
    for t in range(nblk):
        s1, s2 = s_next
        if t + 1 < nblk:
            s_next = scores(t + 1)
        e1 = jnp.exp(s1 - jnp.max(s1, axis=0, keepdims=True)).astype(BF16)
        e2 = jnp.exp(s2 - jnp.max(s2, axis=0, keepdims=True)).astype(BF16)
        u1 = jnp.dot(vt, e1, preferred_element_type=F32)
        u2 = jnp.dot(vt, e2, preferred_element_type=F32)
        o = u1[:hd] * (1.0 / u1[hd:hd + 1]) - u2[:hd] * (lam / u2[hd:hd + 1])
        ms = jnp.mean(o * o, axis=0, keepdims=True)
        o_ref[t * DIFF_TQ:(t + 1) * DIFF_TQ, :] = (
            (o * lax.rsqrt(ms + NORM_EPS) * g).T.astype(o_ref.dtype))


def _diff_attention(a_qk, v_plain, lam_all, subln_all, layer, seq):
    t = a_qk.shape[0]
    kb = A_WIDTH // LANES
    hd = 2 * A_HEAD_DIM
    blk = lambda off: pl.BlockSpec((seq, LANES), lambda b, h: (b, off + h))
    pair = lambda off: pl.BlockSpec((seq, MXU_WIDTH), lambda b, h: (b, off + h // 2))
    return pl.pallas_call(
        functools.partial(_diff_kernel, seq=seq, lam_init=0.8 - 0.6 * math.exp(-0.3 * layer)),
        grid=(t // seq, A_HEADS),
        in_specs=[pair(0), pair(kb // 2), blk(0),
                  pl.BlockSpec((None, 4, A_HEAD_DIM), lambda b, h: (layer, 0, 0)),
                  pl.BlockSpec((None, hd, 1), lambda b, h: (layer, 0, 0))],
        out_specs=pl.BlockSpec((seq, LANES), lambda b, h: (b, h)),
        out_shape=jax.ShapeDtypeStruct((t, A_WIDTH), BF16),
        scratch_shapes=[pltpu.VMEM((hd + BF16_ROWS, seq), BF16)],
        compiler_params=_params("parallel", "parallel"),
        name="diff_attention",
    )(a_qk, a_qk, v_plain, lam_all, subln_all)


def _dil_kernel(*refs, seq):
    q_refs, k_refs, v_refs = refs[0:3], refs[3:6], refs[6:9]
    o_ref = refs[9]
    o_scr, l_scr = refs[10:13], refs[13:16]
    scale = B_HEAD_DIM ** -0.5
    row = lax.broadcasted_iota(jnp.int32, (DIL_TQ, DIL_WIN), 0)
    col = lax.broadcasted_iota(jnp.int32, (DIL_TQ, DIL_WIN), 1)
    rel = col - row

    def step(t, carry):
        q0 = pl.multiple_of(t * DIL_TQ, DIL_TQ)
        ws = pl.multiple_of(jnp.clip(q0 - DIL_HALF, 0, seq - DIL_WIN), DIL_HALF)
        band = jnp.abs(rel + (ws - q0)) <= DIL_HALF
        for gi, (window, d) in enumerate(B_PATTERNS):
            assert window // (2 * d) == DIL_HALF
            n_sub = seq // d
            nblk = n_sub // DIL_TQ
            run = t // nblk
            n = t % nblk
            lo = run * n_sub - ws
            p = _block_class(run, d)
            q = q_refs[gi][pl.ds(q0, DIL_TQ), :]
            k = k_refs[gi][pl.ds(ws, DIL_WIN), :]
            v = v_refs[gi][pl.ds(ws, DIL_WIN), :]
            s = _nt_dot(q, k) * scale
            ok = band & (col >= lo) & (col < lo + n_sub) if d > 1 else band
            s = jnp.where(ok, s, MASK_VALUE)
            m = jnp.max(s, axis=-1, keepdims=True)
            e = jnp.exp(s - m)
            l = jnp.sum(e, axis=-1, keepdims=True)
            o = jnp.dot(e.astype(v.dtype), v, preferred_element_type=F32) / l
            lse = jnp.broadcast_to(m + jnp.log(l), (DIL_TQ, LANES))
            dst = pl.ds(n * (DIL_TQ * d) + p, DIL_TQ, stride=d) if d > 1 else pl.ds(q0, DIL_TQ)
            o_scr[gi][dst, :] = o
            l_scr[gi][dst, :] = lse
        return carry

    lax.fori_loop(0, seq // DIL_TQ, step, 0, unroll=DIL_UNROLL)

    l0, l1, l2 = l_scr[0][...], l_scr[1][...], l_scr[2][...]
    m = jnp.maximum(jnp.maximum(l0, l1), l2)
    e0, e1, e2 = jnp.exp(l0 - m), jnp.exp(l1 - m), jnp.exp(l2 - m)
    num = e0 * o_scr[0][...] + e1 * o_scr[1][...] + e2 * o_scr[2][...]
    o_ref[...] = (num / (e0 + e1 + e2)).astype(o_ref.dtype)


def _dil_attention(v_plain, groups, seq):
    t = v_plain.shape[0]
    blk = lambda off: pl.BlockSpec((seq, LANES), lambda b, h: (b, off + h))
    qs = [qk for qk, _ in groups]
    vs = [v_plain if v is None else v for _, v in groups]
    v_off = [A_WIDTH // LANES if v is None else 0 for _, v in groups]
    in_specs = ([blk(0)] * B_GROUPS + [blk(B_HEADS)] * B_GROUPS + [blk(off) for off in v_off])
    return pl.pallas_call(
        functools.partial(_dil_kernel, seq=seq),
        grid=(t // seq, B_HEADS),
        in_specs=in_specs,
        out_specs=pl.BlockSpec((seq, LANES), lambda b, h: (b, h)),
        out_shape=jax.ShapeDtypeStruct((t, B_OUT), BF16),
        scratch_shapes=[pltpu.VMEM((seq, LANES), F32)] * (2 * B_GROUPS),
        compiler_params=_params("parallel", "parallel"),
        name="dilated_attention",
    )(*qs, *qs, *vs)


def _mix_kernel(oa_ref, ob_ref, ga_ref, gb_ref, x_ref, wpa_ref, wpb_ref, wo_ref, g_ref,
                xo_ref, hn_ref):
    ya = jnp.dot(oa_ref[...], wpa_ref[...], preferred_element_type=F32)
    yb = jnp.dot(ob_ref[...], wpb_ref[...], preferred_element_type=F32)
    merged = (ga_ref[...] * ya + gb_ref[...] * yb).astype(BF16)
    xo = x_ref[...] + jnp.dot(merged, wo_ref[...], preferred_element_type=F32)
    xo_ref[...] = xo
    hn_ref[...] = _rms(xo, g_ref[...]).astype(hn_ref.dtype)


def _mix_out(oa, ob, gates, x, w_pa, w_pb, w_out, norm_ffn, layer):
    t = x.shape[0]
    row = lambda width, col=0: pl.BlockSpec((MIX_TM, width), lambda i: (i, col))
    whole = lambda k, n: pl.BlockSpec((None, k, n), lambda i: (layer, 0, 0),
                                      pipeline_mode=pl.Buffered(1))
    return pl.pallas_call(
        _mix_kernel,
        grid=(t // MIX_TM,),
        in_specs=[row(A_WIDTH), row(B_OUT), row(D_MODEL, 0), row(D_MODEL, 1), row(D_MODEL),
                  whole(A_WIDTH, D_MODEL), whole(B_OUT, D_MODEL), whole(D_MODEL, D_MODEL),
                  pl.BlockSpec((None, 1, D_MODEL), lambda i: (layer, 0, 0))],
        out_specs=[row(D_MODEL), row(D_MODEL)],
        out_shape=[jax.ShapeDtypeStruct((t, D_MODEL), F32),
                   jax.ShapeDtypeStruct((t, D_MODEL), BF16)],
        compiler_params=_params("parallel"),
        name="mix_out",
    )(oa, ob, gates, gates, x, w_pa, w_pb, w_out, norm_ffn)


def _ffn_kernel(hn_ref, wg_ref, wu_ref, w2_ref, x_ref, g_ref, *rest, emit_x):
    outs, acc_ref = rest[:-1], rest[-1]
    k = pl.program_id(1)

    @pl.when(k == 0)
    def _():
        acc_ref[...] = x_ref[...]

    hn = hn_ref[...]
    width = FFN_TF // FFN_CHUNKS
    acts = []
    for c in range(FFN_CHUNKS):
        cols = slice(c * width, (c + 1) * width)
        hg = jnp.dot(hn, wg_ref[:, cols], preferred_element_type=F32)
        hu = jnp.dot(hn, wu_ref[:, cols], preferred_element_type=F32)
        acts.append((hg * _sigmoid(hg) * hu).astype(BF16))
    part = None
    for c in range(FFN_CHUNKS):
        rows = slice(c * width, (c + 1) * width)
        p = jnp.dot(acts[c], w2_ref[rows, :], preferred_element_type=F32)
        part = p if part is None else part + p
    acc_ref[...] += part

    @pl.when(k == pl.num_programs(1) - 1)
    def _():
        xo = acc_ref[...]
        if emit_x:
            outs[0][...] = xo
        outs[-1][...] = _rms(xo, g_ref[...]).astype(outs[-1].dtype)


def _ffn(hn, w_ffn_in, w_ffn_out, x, norm_all, layer, norm_layer, emit_x, tile0=0, n_tiles=None):
    if n_tiles is None:
        n_tiles = x.shape[0] // FFN_TM
    t = n_tiles * FFN_TM
    nk = D_FF // FFN_TF
    row_in = pl.BlockSpec((FFN_TM, D_MODEL), lambda i, k: (tile0 + i, 0))
    row_out = pl.BlockSpec((FFN_TM, D_MODEL), lambda i, k: (i, 0))
    if norm_all.ndim == 3:
        g_spec = pl.BlockSpec((None, 1, D_MODEL), lambda i, k: (norm_layer, 0, 0))
    else:
        g_spec = pl.BlockSpec((1, D_MODEL), lambda i, k: (0, 0))
    out_specs = [row_out]
    out_shape = [jax.ShapeDtypeStruct((t, D_MODEL), BF16 if emit_x else F32)]
    if emit_x:
        out_specs = [row_out, row_out]
        out_shape = [jax.ShapeDtypeStruct((t, D_MODEL), F32)] + out_shape
    return pl.pallas_call(
        functools.partial(_ffn_kernel, emit_x=emit_x),
        grid=(n_tiles, nk),
        in_specs=[row_in,
                  pl.BlockSpec((None, D_MODEL, FFN_TF), lambda i, k: (layer, 0, k)),
                  pl.BlockSpec((None, D_MODEL, FFN_TF), lambda i, k: (layer, 0, nk + k)),
                  pl.BlockSpec((None, FFN_TF, D_MODEL), lambda i, k: (layer, k, 0)),
                  row_in, g_spec],
        out_specs=out_specs,
        out_shape=out_shape,
        scratch_shapes=[pltpu.VMEM((FFN_TM, D_MODEL), F32)],
        compiler_params=_params("parallel", "arbitrary"),
        name="ffn",
    )(hn, w_ffn_in, w_ffn_in, w_ffn_out, x, norm_all)


def _trunk(x_parts, seq, norm_mix, norm_ffn, w_in, gate_bias, diff_lambda, diff_subln,
           w_pa, w_pb, w_out, w_ffn_in, w_ffn_out, norm_final):
    tables = _paired_rope_tables(seq, A_HEAD_DIM) + _split_rope_tables(seq, B_HEAD_DIM)
    x, xn = _rmsnorm(x_parts, norm_mix, 0)
    for layer in range(DEPTH):
        a_qk, v_plain, groups = _qkv_proj(xn, w_in, layer, tables, seq)
        gates = _gate_proj(xn, w_in, gate_bias, layer, seq)
        oa = _diff_attention(a_qk, v_plain, diff_lambda, diff_subln, layer, seq)
        ob = _dil_attention(v_plain, groups, seq)
        x, hn = _mix_out(oa, ob, gates, x, w_pa, w_pb, w_out, norm_ffn, layer)
        if layer + 1 < DEPTH:
            x, xn = _ffn(hn, w_ffn_in, w_ffn_out, x, norm_mix, layer, layer + 1, True)
    outs, tile0 = [], 0
    for part in x_parts:
        n_tiles = part.shape[0] // FFN_TM
        (y,) = _ffn(hn, w_ffn_in, w_ffn_out, x, norm_final, DEPTH - 1, 0, False, tile0, n_tiles)
        outs.append(y)
        tile0 += n_tiles
    return outs


def kernel(x_prompt, x_sample, norm_mix, norm_ffn, w_in, gate_bias, diff_lambda, diff_subln,
           w_proj_a, w_proj_b, w_out, w_ffn_in, w_ffn_out, norm_final):
    bp, seq, d = x_prompt.shape
    bs = x_sample.shape[0]
    assert x_sample.shape[1:] == (seq, d) and d == D_MODEL
    assert seq % (B_PATTERNS[-1][1] * DIL_TQ) == 0 and seq >= DIL_WIN
    y_prompt, y_sample = _trunk(
        [x_prompt.reshape(bp * seq, d), x_sample.reshape(bs * seq, d)], seq,
        norm_mix.reshape(DEPTH, 1, D_MODEL), norm_ffn.reshape(DEPTH, 1, D_MODEL),
        _prepare_w_in(w_in), gate_bias.reshape(DEPTH, 1, GATE_WIDTH),
        diff_lambda, diff_subln.reshape(DEPTH, 2 * A_HEAD_DIM, 1),
        w_proj_a.astype(BF16), w_proj_b.astype(BF16), w_out.astype(BF16),
        w_ffn_in.astype(BF16), w_ffn_out.astype(BF16), norm_final.reshape(1, D_MODEL))
    return (y_prompt.reshape(bp, seq, d), y_sample.reshape(bs, seq, d))
```

```python
import functools
import math

import jax
import jax.numpy as jnp
from jax import lax
from jax.experimental import pallas as pl
from jax.experimental.pallas import tpu as pltpu

D_MODEL = 2048
DEPTH = 4
A_HEADS = 8
A_HEAD_DIM = 64
A_WIDTH = A_HEADS * 2 * A_HEAD_DIM
B_PATTERNS = ((128, 1), (512, 4), (2048, 16))
B_GROUPS = len(B_PATTERNS)
B_HEADS = 4
B_HEAD_DIM = 128
B_WIDTH = B_GROUPS * B_HEADS * B_HEAD_DIM
B_OUT = B_HEADS * B_HEAD_DIM
QKV_WIDTH = 3 * A_WIDTH + 3 * B_WIDTH
GATE_WIDTH = 2 * D_MODEL
D_FF = 5632
ROPE_THETA = 500000.0
ROPE_FRAC = 4
NORM_EPS = 1e-6
MASK_VALUE = -1e30

LANES = 128
BF16_ROWS = 16
MXU_WIDTH = 256
VMEM_LIMIT = 60 * 1024 * 1024

PROJ_TN = 512
DIFF_TQ = 512
DIL_UNROLL = 4
DIL_TQ = 128
DIL_HALF = 64
DIL_WIN = 256
MAX_ROW_STRIDE = 4
MIX_TM = 512
FFN_TM = 512
FFN_TF = 512
FFN_CHUNKS = 2
NORM_TM = 512

BF16 = jnp.bfloat16
F32 = jnp.float32


def _params(*sem):
    return pltpu.CompilerParams(dimension_semantics=sem, vmem_limit_bytes=VMEM_LIMIT)


def _sigmoid(x):
    return 0.5 * jnp.tanh(0.5 * x) + 0.5


def _rms(x, g):
    ms = jnp.mean(x * x, axis=-1, keepdims=True)
    return x * lax.rsqrt(ms + NORM_EPS) * g


def _stream_specs(parts, tm):
    specs, firsts, start = [], [], 0
    for arr in parts:
        n = arr.shape[0] // tm
        specs.append(pl.BlockSpec((tm, D_MODEL),
                                  lambda i, *_, start=start, n=n: (jnp.clip(i - start, 0, n - 1), 0)))
        firsts.append(start)
        start += n
    return specs, tuple(firsts)


def _stream_tile(refs, firsts, i):
    x = refs[0][...]
    for ref, first in zip(refs[1:], firsts[1:]):
        x = jnp.where(i >= first, ref[...], x)
    return x


def _rmsnorm_kernel(*refs, firsts):
    x_refs, (g_ref, x_ref, o_ref) = refs[:-3], refs[-3:]
    x = _stream_tile(x_refs, firsts, pl.program_id(0))
    x_ref[...] = x
    o_ref[...] = _rms(x, g_ref[...]).astype(o_ref.dtype)


def _rmsnorm(parts, g_all, layer):
    t = sum(p.shape[0] for p in parts)
    specs, firsts = _stream_specs(parts, NORM_TM)
    row = pl.BlockSpec((NORM_TM, D_MODEL), lambda i: (i, 0))
    return pl.pallas_call(
        functools.partial(_rmsnorm_kernel, firsts=firsts),
        grid=(t // NORM_TM,),
        in_specs=specs + [pl.BlockSpec((None, 1, D_MODEL), lambda i: (layer, 0, 0))],
        out_specs=[row, row],
        out_shape=[jax.ShapeDtypeStruct((t, D_MODEL), F32), jax.ShapeDtypeStruct((t, D_MODEL), BF16)],
        compiler_params=_params("parallel"),
        name="rmsnorm_in",
    )(*parts, g_all)


def _stride_stages(d):
    if d <= MAX_ROW_STRIDE:
        return d, 1
    assert d % MAX_ROW_STRIDE == 0 and d // MAX_ROW_STRIDE <= MAX_ROW_STRIDE
    return MAX_ROW_STRIDE, d // MAX_ROW_STRIDE


def _block_class(blk, d):
    s1, s2 = _stride_stages(d)
    return blk if s2 == 1 else s1 * (blk % s2) + blk // s2


def _proj_kernel(x_ref, w_ref, *rest, seq, rope, d, tn, scaled_tiles):
    if rope:
        c_ref, s_ref, o_ref, *scr = rest
    else:
        o_ref, *scr = rest
    j = pl.program_id(1)
    n_sub = seq // d
    x = x_ref[...]
    assert MXU_WIDTH == 2 * LANES
    for m in range(tn // MXU_WIDTH):
        acc = jnp.dot(x, w_ref[:, m * MXU_WIDTH:(m + 1) * MXU_WIDTH], preferred_element_type=F32)
        halves = [acc[:, :LANES], acc[:, LANES:]]
        if rope == "near":
            half = A_HEAD_DIM // ROPE_FRAC // 2
            first = lax.broadcasted_iota(jnp.int32, halves[0].shape, 1) % A_HEAD_DIM < half
            halves = [h * c_ref[...] + jnp.where(first, pltpu.roll(h, LANES - half, 1),
                                                 pltpu.roll(h, half, 1)) * s_ref[...]
                      for h in halves]
        elif rope == "split":
            halves = [h * c_ref[...] + pltpu.roll(h, LANES // 2, 1) * s_ref[...] for h in halves]
        for c, val in enumerate(halves):
            chunk = m * (MXU_WIDTH // LANES) + c
            sl = slice(chunk * LANES, (chunk + 1) * LANES)
            if scaled_tiles:
                val = val * jnp.where(j < scaled_tiles, A_HEAD_DIM ** -0.5, 1.0).astype(F32)
            if d == 1:
                o_ref[:, sl] = val.astype(o_ref.dtype)
                continue
            buf = scr[chunk]
            buf[...] = val
            s1, s2 = _stride_stages(d)
            if s2 > 1:
                tmp, n1 = scr[tn // LANES + chunk], seq // s1
                for p1 in range(s1):
                    tmp[p1 * n1:(p1 + 1) * n1, :] = buf[pl.ds(p1, n1, stride=s1), :]
                runs = [(tmp, p1 * n1, s2) for p1 in range(s1)]
            else:
                runs = [(buf, 0, s1)]
            blk = 0
            for src, base, stride in runs:
                for p in range(stride):
                    o_ref[blk * n_sub:(blk + 1) * n_sub, sl] = (
                        src[pl.ds(base + p, n_sub, stride=stride), :].astype(o_ref.dtype))
                    blk += 1


def _proj_class(xn, w_in, layer, seq, name, tiles, tn, rope=None, tables=None, d=1, scaled_tiles=0):
    t = xn.shape[0]

    def w_map(i, j):
        tile = tiles[0]
        for idx in range(1, len(tiles)):
            tile = jnp.where(j >= idx, tiles[idx], tile)
        return (layer, 0, tile)

    in_specs = [pl.BlockSpec((seq, D_MODEL), lambda i, j: (i, 0)),
                pl.BlockSpec((None, D_MODEL, tn), w_map)]
    args = [xn, w_in]
    if rope:
        in_specs += [pl.BlockSpec((seq, LANES), lambda i, j: (0, 0))] * 2
        args += list(tables)
    return pl.pallas_call(
        functools.partial(_proj_kernel, seq=seq, rope=rope, d=d, tn=tn, scaled_tiles=scaled_tiles),
        grid=(t // seq, len(tiles)),
        in_specs=in_specs,
        out_specs=pl.BlockSpec((seq, tn), lambda i, j: (i, j)),
        out_shape=jax.ShapeDtypeStruct((t, len(tiles) * tn), BF16),
        scratch_shapes=([pltpu.VMEM((seq, LANES), F32)]
                        * (tn // LANES * (0 if d == 1 else 1 if _stride_stages(d)[1] == 1 else 2))),
        compiler_params=_params("parallel", "arbitrary"),
        name=name,
    )(*args)


def _qkv_proj(xn, w_in, layer, tables, seq):
    tab_a, tab_b = tables[:2], tables[2:]
    wide = 2 * PROJ_TN
    n_a = 3 * A_WIDTH // PROJ_TN
    n_b = B_WIDTH // PROJ_TN
    assert n_b == B_GROUPS and A_WIDTH == wide
    proj = functools.partial(_proj_class, xn, w_in, layer, seq)
    a_qk = proj("proj_a_qk", (0, 1), wide, rope="near", tables=tab_a, scaled_tiles=1)
    v_plain = proj("proj_v_plain", (n_a - 2, n_a - 1, n_a + 2 * n_b), PROJ_TN)
    groups = []
    for gi, (_, d) in enumerate(B_PATTERNS):
        qk = proj(f"proj_b_qk_d{d}", (n_a + gi, n_a + n_b + gi), PROJ_TN,
                  rope="split", tables=tab_b, d=d)
        v = None if d == 1 else proj(f"proj_b_v_d{d}", (n_a + 2 * n_b + gi,), PROJ_TN, d=d)
        groups.append((qk, v))
    return a_qk, v_plain, groups


def _near_rope_tables(seq, head_dim):
    rot = head_dim // ROPE_FRAC
    inv = ROPE_THETA ** (-(jnp.arange(0, rot, 2, dtype=F32) / rot))
    ang = jnp.arange(seq, dtype=F32)[:, None] * inv[None, :]
    cos, sin = jnp.cos(ang), jnp.sin(ang)
    rest = head_dim - rot
    c = jnp.concatenate([cos, cos, jnp.ones((seq, rest), F32)], axis=-1)
    s = jnp.concatenate([-sin, sin, jnp.zeros((seq, rest), F32)], axis=-1)
    reps = LANES // head_dim
    return jnp.tile(c, (1, reps)), jnp.tile(s, (1, reps))


def _split_order(head_dim):
    assert head_dim == LANES
    half = head_dim // ROPE_FRAC // 2
    keep = LANES // 2 - half
    old = list(range(head_dim))
    x1, x2, rest = old[:half], old[half:2 * half], old[2 * half:]
    return x1 + rest[:keep] + x2 + rest[keep:]


def _split_rope_tables(seq, head_dim):
    rot = head_dim // ROPE_FRAC
    inv = ROPE_THETA ** (-(jnp.arange(0, rot, 2, dtype=F32) / rot))
    ang = jnp.arange(seq, dtype=F32)[:, None] * inv[None, :]
    cos, sin = jnp.cos(ang), jnp.sin(ang)
    rest = LANES // 2 - rot // 2
    one, zero = jnp.ones((seq, rest), F32), jnp.zeros((seq, rest), F32)
    return (jnp.concatenate([cos, one, cos, one], axis=-1),
            jnp.concatenate([-sin, zero, sin, zero], axis=-1))


def _w_in_kernel(w_ref, p_ref, o_ref):
    j = pl.program_id(1)
    n_a = 3 * A_WIDTH // PROJ_TN
    n_b = B_WIDTH // PROJ_TN
    reorder = (j >= n_a) & (j < n_a + 2 * n_b)
    w = w_ref[...].astype(BF16)

    @pl.when(reorder)
    def _():
        for m in range(PROJ_TN // MXU_WIDTH):
            cols = slice(m * MXU_WIDTH, (m + 1) * MXU_WIDTH)
            o_ref[:, cols] = jnp.dot(w[:, cols], p_ref[...],
                                     preferred_element_type=F32).astype(o_ref.dtype)

    @pl.when(jnp.logical_not(reorder))
    def _():
        o_ref[...] = w


def _prepare_w_in(w_in):
    depth, k, n = w_in.shape
    order = _split_order(B_HEAD_DIM)
    perm = [[0.0] * MXU_WIDTH for _ in range(MXU_WIDTH)]
    for head in range(MXU_WIDTH // B_HEAD_DIM):
        for new, old in enumerate(order):
            perm[head * B_HEAD_DIM + old][head * B_HEAD_DIM + new] = 1.0
    return pl.pallas_call(
        _w_in_kernel,
        grid=(depth, n // PROJ_TN),
        in_specs=[pl.BlockSpec((None, k, PROJ_TN), lambda l, j: (l, 0, j)),
                  pl.BlockSpec((MXU_WIDTH, MXU_WIDTH), lambda l, j: (0, 0))],
        out_specs=pl.BlockSpec((None, k, PROJ_TN), lambda l, j: (l, 0, j)),
        out_shape=jax.ShapeDtypeStruct(w_in.shape, BF16),
        compiler_params=_params("parallel", "parallel"),
        name="w_in_prepare",
    )(w_in, jnp.asarray(perm, BF16))


def _gate_kernel(x_ref, w_ref, b_ref, o_ref):
    x = x_ref[...]
    for m in range(PROJ_TN // MXU_WIDTH):
        cols = slice(m * MXU_WIDTH, (m + 1) * MXU_WIDTH)
        z = jnp.dot(x, w_ref[:, cols], preferred_element_type=F32) + b_ref[:, cols]
        o_ref[:, cols] = _sigmoid(z)


def _gate_proj(xn, w_in, bias, layer, seq):
    t = xn.shape[0]
    col0 = QKV_WIDTH // PROJ_TN
    return pl.pallas_call(
        _gate_kernel,
        grid=(t // seq, GATE_WIDTH // PROJ_TN),
        in_specs=[pl.BlockSpec((seq, D_MODEL), lambda i, j: (i, 0)),
                  pl.BlockSpec((None, D_MODEL, PROJ_TN), lambda i, j: (layer, 0, col0 + j)),
                  pl.BlockSpec((None, 1, PROJ_TN), lambda i, j: (layer, 0, j))],
        out_specs=pl.BlockSpec((seq, PROJ_TN), lambda i, j: (i, j)),
        out_shape=jax.ShapeDtypeStruct((t, GATE_WIDTH), F32),
        compiler_params=_params("parallel", "arbitrary"),
        name="gate_proj",
    )(xn, w_in, bias)


def _nt_dot(a, b):
    return lax.dot_general(a, b, (((1,), (1,)), ((), ())), preferred_element_type=F32)


def _diff_kernel(q_ref, k_ref, v_ref, lam_ref, g_ref, o_ref, vt_ref, *, seq, lam_init):
    lf = lam_ref[...]
    lam = (jnp.exp(jnp.sum(lf[0:1] * lf[1:2], axis=-1, keepdims=True))
           - jnp.exp(jnp.sum(lf[2:3] * lf[3:4], axis=-1, keepdims=True)) + lam_init)
    hd = 2 * A_HEAD_DIM
    k = k_ref[...]
    vt_ref[0:hd, :] = v_ref[...].astype(F32).T.astype(BF16)
    vt_ref[hd:, :] = jnp.ones((vt_ref.shape[0] - hd, seq), BF16)
    vt = vt_ref[...]
    g = g_ref[...] * (1.0 - lam_init)
    lane = lax.broadcasted_iota(jnp.int32, (DIFF_TQ, LANES), 1)
    first = lane < A_HEAD_DIM

    def scores(t):
        q = q_ref[t * DIFF_TQ:(t + 1) * DIFF_TQ, :]
        zero = jnp.zeros_like(q)
        return (_nt_dot(k, jnp.where(first, q, zero)),
                _nt_dot(k, jnp.where(first, zero, q)))

    nblk = seq // DIFF_TQ
    s_next = scores(0)
    for t in range(nblk):
        s1, s2 = s_next
        if t + 1 < nblk:
            s_next = scores(t + 1)
        e1 = jnp.exp(s1 - jnp.max(s1, axis=0, keepdims=True)).astype(BF16)
        e2 = jnp.exp(s2 - jnp.max(s2, axis=0, keepdims=True)).astype(BF16)
        u1 = jnp.dot(vt, e1, preferred_element_type=F32)
        u2 = jnp.dot(vt, e2, preferred_element_type=F32)
        o = u1[:hd] * (1.0 / u1[hd:hd + 1]) - u2[:hd] * (lam / u2[hd:hd + 1])
        ms = jnp.mean(o * o, axis=0, keepdims=True)
        o_ref[t * DIFF_TQ:(t + 1) * DIFF_TQ, :] = (
            (o * lax.rsqrt(ms + NORM_EPS) * g).T.astype(o_ref.dtype))


def _diff_attention(a_qk, v_plain, lam_all, subln_all, layer, seq):
    t = a_qk.shape[0]
    kb = A_WIDTH // LANES
    hd = 2 * A_HEAD_DIM
    blk = lambda off: pl.BlockSpec((seq, LANES), lambda b, h: (b, off + h))
    return pl.pallas_call(
        functools.partial(_diff_kernel, seq=seq, lam_init=0.8 - 0.6 * math.exp(-0.3 * layer)),
        grid=(t // seq, A_HEADS),
        in_specs=[blk(0), blk(kb), blk(0),
                  pl.BlockSpec((None, 4, A_HEAD_DIM), lambda b, h: (layer, 0, 0)),
                  pl.BlockSpec((None, hd, 1), lambda b, h: (layer, 0, 0))],
        out_specs=pl.BlockSpec((seq, LANES), lambda b, h: (b, h)),
        out_shape=jax.ShapeDtypeStruct((t, A_WIDTH), BF16),
        scratch_shapes=[pltpu.VMEM((hd + BF16_ROWS, seq), BF16)],
        compiler_params=_params("parallel", "parallel"),
        name="diff_attention",
    )(a_qk, a_qk, v_plain, lam_all, subln_all)


def _dil_kernel(*refs, seq):
    q_refs, k_refs, v_refs = refs[0:3], refs[3:6], refs[6:9]
    o_ref = refs[9]
    o_scr, l_scr = refs[10:13], refs[13:16]
    scale = B_HEAD_DIM ** -0.5
    row = lax.broadcasted_iota(jnp.int32, (DIL_TQ, DIL_WIN), 0)
    col = lax.broadcasted_iota(jnp.int32, (DIL_TQ, DIL_WIN), 1)
    rel = col - row

    def step(t, carry):
        q0 = pl.multiple_of(t * DIL_TQ, DIL_TQ)
        ws = pl.multiple_of(jnp.clip(q0 - DIL_HALF, 0, seq - DIL_WIN), DIL_HALF)
        band = jnp.abs(rel + (ws - q0)) <= DIL_HALF
        for gi, (window, d) in enumerate(B_PATTERNS):
            assert window // (2 * d) == DIL_HALF
            n_sub = seq // d
            nblk = n_sub // DIL_TQ
            run = t // nblk
            n = t % nblk
            lo = run * n_sub - ws
            p = _block_class(run, d)
            q = q_refs[gi][pl.ds(q0, DIL_TQ), :]
            k = k_refs[gi][pl.ds(ws, DIL_WIN), :]
            v = v_refs[gi][pl.ds(ws, DIL_WIN), :]
            s = _nt_dot(q, k) * scale
            ok = band & (col >= lo) & (col < lo + n_sub) if d > 1 else band
            s = jnp.where(ok, s, MASK_VALUE)
            m = jnp.max(s, axis=-1, keepdims=True)
            e = jnp.exp(s - m)
            l = jnp.sum(e, axis=-1, keepdims=True)
            o = jnp.dot(e.astype(v.dtype), v, preferred_element_type=F32) / l
            lse = jnp.broadcast_to(m + jnp.log(l), (DIL_TQ, LANES))
            dst = pl.ds(n * (DIL_TQ * d) + p, DIL_TQ, stride=d) if d > 1 else pl.ds(q0, DIL_TQ)
            o_scr[gi][dst, :] = o
            l_scr[gi][dst, :] = lse
        return carry

    lax.fori_loop(0, seq // DIL_TQ, step, 0, unroll=DIL_UNROLL)

    l0, l1, l2 = l_scr[0][...], l_scr[1][...], l_scr[2][...]
    m = jnp.maximum(jnp.maximum(l0, l1), l2)
    e0, e1, e2 = jnp.exp(l0 - m), jnp.exp(l1 - m), jnp.exp(l2 - m)
    num = e0 * o_scr[0][...] + e1 * o_scr[1][...] + e2 * o_scr[2][...]
    o_ref[...] = (num / (e0 + e1 + e2)).astype(o_ref.dtype)


def _dil_attention(v_plain, groups, seq):
    t = v_plain.shape[0]
    blk = lambda off: pl.BlockSpec((seq, LANES), lambda b, h: (b, off + h))
    qs = [qk for qk, _ in groups]
    vs = [v_plain if v is None else v for _, v in groups]
    v_off = [A_WIDTH // LANES if v is None else 0 for _, v in groups]
    in_specs = ([blk(0)] * B_GROUPS + [blk(B_HEADS)] * B_GROUPS + [blk(off) for off in v_off])
    return pl.pallas_call(
        functools.partial(_dil_kernel, seq=seq),
        grid=(t // seq, B_HEADS),
        in_specs=in_specs,
        out_specs=pl.BlockSpec((seq, LANES), lambda b, h: (b, h)),
        out_shape=jax.ShapeDtypeStruct((t, B_OUT), BF16),
        scratch_shapes=[pltpu.VMEM((seq, LANES), F32)] * (2 * B_GROUPS),
        compiler_params=_params("parallel", "parallel"),
        name="dilated_attention",
    )(*qs, *qs, *vs)


def _mix_kernel(oa_ref, ob_ref, ga_ref, gb_ref, x_ref, wpa_ref, wpb_ref, wo_ref, g_ref,
                xo_ref, hn_ref):
    ya = jnp.dot(oa_ref[...], wpa_ref[...], preferred_element_type=F32)
    yb = jnp.dot(ob_ref[...], wpb_ref[...], preferred_element_type=F32)
    merged = (ga_ref[...] * ya + gb_ref[...] * yb).astype(BF16)
    xo = x_ref[...] + jnp.dot(merged, wo_ref[...], preferred_element_type=F32)
    xo_ref[...] = xo
    hn_ref[...] = _rms(xo, g_ref[...]).astype(hn_ref.dtype)


def _mix_out(oa, ob, gates, x, w_pa, w_pb, w_out, norm_ffn, layer):
    t = x.shape[0]
    row = lambda width, col=0: pl.BlockSpec((MIX_TM, width), lambda i: (i, col))
    whole = lambda k, n: pl.BlockSpec((None, k, n), lambda i: (layer, 0, 0),
                                      pipeline_mode=pl.Buffered(1))
    return pl.pallas_call(
        _mix_kernel,
        grid=(t // MIX_TM,),
        in_specs=[row(A_WIDTH), row(B_OUT), row(D_MODEL, 0), row(D_MODEL, 1), row(D_MODEL),
                  whole(A_WIDTH, D_MODEL), whole(B_OUT, D_MODEL), whole(D_MODEL, D_MODEL),
                  pl.BlockSpec((None, 1, D_MODEL), lambda i: (layer, 0, 0))],
        out_specs=[row(D_MODEL), row(D_MODEL)],
        out_shape=[jax.ShapeDtypeStruct((t, D_MODEL), F32),
                   jax.ShapeDtypeStruct((t, D_MODEL), BF16)],
        compiler_params=_params("parallel"),
        name="mix_out",
    )(oa, ob, gates, gates, x, w_pa, w_pb, w_out, norm_ffn)


def _ffn_kernel(hn_ref, wg_ref, wu_ref, w2_ref, x_ref, g_ref, *rest, emit_x):
    outs, acc_ref = rest[:-1], rest[-1]
    k = pl.program_id(1)

    @pl.when(k == 0)
    def _():
        acc_ref[...] = x_ref[...]

    hn = hn_ref[...]
    width = FFN_TF // FFN_CHUNKS
    acts = []
    for c in range(FFN_CHUNKS):
        cols = slice(c * width, (c + 1) * width)
        hg = jnp.dot(hn, wg_ref[:, cols], preferred_element_type=F32)
        hu = jnp.dot(hn, wu_ref[:, cols], preferred_element_type=F32)
        acts.append((hg * _sigmoid(hg) * hu).astype(BF16))
    part = None
    for c in range(FFN_CHUNKS):
        rows = slice(c * width, (c + 1) * width)
        p = jnp.dot(acts[c], w2_ref[rows, :], preferred_element_type=F32)
        part = p if part is None else part + p
    acc_ref[...] += part

    @pl.when(k == pl.num_programs(1) - 1)
    def _():
        xo = acc_ref[...]
        if emit_x:
            outs[0][...] = xo
        outs[-1][...] = _rms(xo, g_ref[...]).astype(outs[-1].dtype)


def _ffn(hn, w_ffn_in, w_ffn_out, x, norm_all, layer, norm_layer, emit_x, tile0=0, n_tiles=None):
    if n_tiles is None:
        n_tiles = x.shape[0] // FFN_TM
    t = n_tiles * FFN_TM
    nk = D_FF // FFN_TF
    row_in = pl.BlockSpec((FFN_TM, D_MODEL), lambda i, k: (tile0 + i, 0))
    row_out = pl.BlockSpec((FFN_TM, D_MODEL), lambda i, k: (i, 0))
    if norm_all.ndim == 3:
        g_spec = pl.BlockSpec((None, 1, D_MODEL), lambda i, k: (norm_layer, 0, 0))
    else:
        g_spec = pl.BlockSpec((1, D_MODEL), lambda i, k: (0, 0))
    out_specs = [row_out]
    out_shape = [jax.ShapeDtypeStruct((t, D_MODEL), BF16 if emit_x else F32)]
    if emit_x:
        out_specs = [row_out, row_out]
        out_shape = [jax.ShapeDtypeStruct((t, D_MODEL), F32)] + out_shape
    return pl.pallas_call(
        functools.partial(_ffn_kernel, emit_x=emit_x),
        grid=(n_tiles, nk),
        in_specs=[row_in,
                  pl.BlockSpec((None, D_MODEL, FFN_TF), lambda i, k: (layer, 0, k)),
                  pl.BlockSpec((None, D_MODEL, FFN_TF), lambda i, k: (layer, 0, nk + k)),
                  pl.BlockSpec((None, FFN_TF, D_MODEL), lambda i, k: (layer, k, 0)),
                  row_in, g_spec],
        out_specs=out_specs,
        out_shape=out_shape,
        scratch_shapes=[pltpu.VMEM((FFN_TM, D_MODEL), F32)],
        compiler_params=_params("parallel", "arbitrary"),
        name="ffn",
    )(hn, w_ffn_in, w_ffn_in, w_ffn_out, x, norm_all)


def _trunk(x_parts, seq, norm_mix, norm_ffn, w_in, gate_bias, diff_lambda, diff_subln,
           w_pa, w_pb, w_out, w_ffn_in, w_ffn_out, norm_final):
    tables = _near_rope_tables(seq, A_HEAD_DIM) + _split_rope_tables(seq, B_HEAD_DIM)
    x, xn = _rmsnorm(x_parts, norm_mix, 0)
    for layer in range(DEPTH):
        a_qk, v_plain, groups = _qkv_proj(xn, w_in, layer, tables, seq)
        gates = _gate_proj(xn, w_in, gate_bias, layer, seq)
        oa = _diff_attention(a_qk, v_plain, diff_lambda, diff_subln, layer, seq)
        ob = _dil_attention(v_plain, groups, seq)
        x, hn = _mix_out(oa, ob, gates, x, w_pa, w_pb, w_out, norm_ffn, layer)
        if layer + 1 < DEPTH:
            x, xn = _ffn(hn, w_ffn_in, w_ffn_out, x, norm_mix, layer, layer + 1, True)
    outs, tile0 = [], 0
    for part in x_parts:
        n_tiles = part.shape[0] // FFN_TM
        (y,) = _ffn(hn, w_ffn_in, w_ffn_out, x, norm_final, DEPTH - 1, 0, False, tile0, n_tiles)
        outs.append(y)
        tile0 += n_tiles
    return outs


def kernel(x_prompt, x_sample, norm_mix, norm_ffn, w_in, gate_bias, diff_lambda, diff_subln,
           w_proj_a, w_proj_b, w_out, w_ffn_in, w_ffn_out, norm_final):
    bp, seq, d = x_prompt.shape
    bs = x_sample.shape[0]
    assert x_sample.shape[1:] == (seq, d) and d == D_MODEL
    assert seq % (B_PATTERNS[-1][1] * DIL_TQ) == 0 and seq >= DIL_WIN
    y_prompt, y_sample = _trunk(
        [x_prompt.reshape(bp * seq, d), x_sample.reshape(bs * seq, d)], seq,
        norm_mix.reshape(DEPTH, 1, D_MODEL), norm_ffn.reshape(DEPTH, 1, D_MODEL),
        _prepare_w_in(w_in), gate_bias.reshape(DEPTH, 1, GATE_WIDTH),
        diff_lambda, diff_subln.reshape(DEPTH, 2 * A_HEAD_DIM, 1),
        w_proj_a.astype(BF16), w_proj_b.astype(BF16), w_out.astype(BF16),
        w_ffn_in.astype(BF16), w_ffn_out.astype(BF16), norm_final.reshape(1, D_MODEL))
    return (y_prompt.reshape(bp, seq, d), y_sample.reshape(bs, seq, d))
```

```python
import functools
import math

import jax
import jax.numpy as jnp
from jax import lax
from jax.experimental import pallas as pl
from jax.experimental.pallas import tpu as pltpu

D_MODEL = 2048
DEPTH = 4
A_HEADS = 8
A_HEAD_DIM = 64
A_WIDTH = A_HEADS * 2 * A_HEAD_DIM
B_PATTERNS = ((128, 1), (512, 4), (2048, 16))
B_GROUPS = len(B_PATTERNS)
B_HEADS = 4
B_HEAD_DIM = 128
B_WIDTH = B_GROUPS * B_HEADS * B_HEAD_DIM
B_OUT = B_HEADS * B_HEAD_DIM
QKV_WIDTH = 3 * A_WIDTH + 3 * B_WIDTH
GATE_WIDTH = 2 * D_MODEL
D_FF = 5632
ROPE_THETA = 500000.0
ROPE_FRAC = 4
NORM_EPS = 1e-6
MASK_VALUE = -1e30

LANES = 128
BF16_ROWS = 16
MXU_WIDTH = 256
VMEM_LIMIT = 60 * 1024 * 1024

PROJ_TN = 512
DIFF_TQ = 512
DIL_UNROLL = 4
DIL_TQ = 128
DIL_HALF = 64
DIL_WIN = 256
MAX_ROW_STRIDE = 4
MIX_TM = 512
FFN_TM = 512
FFN_TM_WIDE = 768
FFN_TF = 512
FFN_CHUNKS = 2
NORM_TM = 512

BF16 = jnp.bfloat16
F32 = jnp.float32


def _params(*sem):
    return pltpu.CompilerParams(dimension_semantics=sem, vmem_limit_bytes=VMEM_LIMIT)


def _sigmoid(x):
    return 0.5 * jnp.tanh(0.5 * x) + 0.5


def _rms(x, g):
    ms = jnp.mean(x * x, axis=-1, keepdims=True)
    return x * lax.rsqrt(ms + NORM_EPS) * g


def _stream_specs(parts, tm):
    specs, firsts, start = [], [], 0
    for arr in parts:
        n = arr.shape[0] // tm
        specs.append(pl.BlockSpec((tm, D_MODEL),
                                  lambda i, *_, start=start, n=n: (jnp.clip(i - start, 0, n - 1), 0)))
        firsts.append(start)
        start += n
    return specs, tuple(firsts)


def _stream_tile(refs, firsts, i):
    x = refs[0][...]
    for ref, first in zip(refs[1:], firsts[1:]):
        x = jnp.where(i >= first, ref[...], x)
    return x


def _rmsnorm_kernel(*refs, firsts):
    x_refs, (g_ref, x_ref, o_ref) = refs[:-3], refs[-3:]
    x = _stream_tile(x_refs, firsts, pl.program_id(0))
    x_ref[...] = x
    o_ref[...] = _rms(x, g_ref[...]).astype(o_ref.dtype)


def _rmsnorm(parts, g_all, layer):
    t = sum(p.shape[0] for p in parts)
    specs, firsts = _stream_specs(parts, NORM_TM)
    row = pl.BlockSpec((NORM_TM, D_MODEL), lambda i: (i, 0))
    return pl.pallas_call(
        functools.partial(_rmsnorm_kernel, firsts=firsts),
        grid=(t // NORM_TM,),
        in_specs=specs + [pl.BlockSpec((None, 1, D_MODEL), lambda i: (layer, 0, 0))],
        out_specs=[row, row],
        out_shape=[jax.ShapeDtypeStruct((t, D_MODEL), F32), jax.ShapeDtypeStruct((t, D_MODEL), BF16)],
        compiler_params=_params("parallel"),
        name="rmsnorm_in",
    )(*parts, g_all)


def _stride_stages(d):
    if d <= MAX_ROW_STRIDE:
        return d, 1
    assert d % MAX_ROW_STRIDE == 0 and d // MAX_ROW_STRIDE <= MAX_ROW_STRIDE
    return MAX_ROW_STRIDE, d // MAX_ROW_STRIDE


def _block_class(blk, d):
    s1, s2 = _stride_stages(d)
    return blk if s2 == 1 else s1 * (blk % s2) + blk // s2


def _proj_kernel(x_ref, w_ref, *rest, seq, rope, d, tn, scaled_tiles):
    if rope:
        c_ref, s_ref, o_ref, *scr = rest
    else:
        o_ref, *scr = rest
    j = pl.program_id(1)
    n_sub = seq // d
    x = x_ref[...]
    assert MXU_WIDTH == 2 * LANES
    for m in range(tn // MXU_WIDTH):
        acc = jnp.dot(x, w_ref[:, m * MXU_WIDTH:(m + 1) * MXU_WIDTH], preferred_element_type=F32)
        halves = [acc[:, :LANES], acc[:, LANES:]]
        if rope:
            halves = [h * c_ref[...] + pltpu.roll(h, LANES // 2, 1) * s_ref[...] for h in halves]
        for c, val in enumerate(halves):
            chunk = m * (MXU_WIDTH // LANES) + c
            sl = slice(chunk * LANES, (chunk + 1) * LANES)
            if scaled_tiles:
                val = val * jnp.where(j < scaled_tiles, A_HEAD_DIM ** -0.5, 1.0).astype(F32)
            if d == 1:
                o_ref[:, sl] = val.astype(o_ref.dtype)
                continue
            buf = scr[chunk]
            buf[...] = val
            s1, s2 = _stride_stages(d)
            if s2 > 1:
                tmp, n1 = scr[tn // LANES + chunk], seq // s1
                for p1 in range(s1):
                    tmp[p1 * n1:(p1 + 1) * n1, :] = buf[pl.ds(p1, n1, stride=s1), :]
                runs = [(tmp, p1 * n1, s2) for p1 in range(s1)]
            else:
                runs = [(buf, 0, s1)]
            blk = 0
            for src, base, stride in runs:
                for p in range(stride):
                    o_ref[blk * n_sub:(blk + 1) * n_sub, sl] = (
                        src[pl.ds(base + p, n_sub, stride=stride), :].astype(o_ref.dtype))
                    blk += 1


def _proj_class(xn, w_in, layer, seq, name, tiles, tn, rope=None, tables=None, d=1, scaled_tiles=0):
    t = xn.shape[0]

    def w_map(i, j):
        tile = tiles[0]
        for idx in range(1, len(tiles)):
            tile = jnp.where(j >= idx, tiles[idx], tile)
        return (layer, 0, tile)

    in_specs = [pl.BlockSpec((seq, D_MODEL), lambda i, j: (i, 0)),
                pl.BlockSpec((None, D_MODEL, tn), w_map)]
    args = [xn, w_in]
    if rope:
        in_specs += [pl.BlockSpec((seq, LANES), lambda i, j: (0, 0))] * 2
        args += list(tables)
    return pl.pallas_call(
        functools.partial(_proj_kernel, seq=seq, rope=rope, d=d, tn=tn, scaled_tiles=scaled_tiles),
        grid=(t // seq, len(tiles)),
        in_specs=in_specs,
        out_specs=pl.BlockSpec((seq, tn), lambda i, j: (i, j)),
        out_shape=jax.ShapeDtypeStruct((t, len(tiles) * tn), BF16),
        scratch_shapes=([pltpu.VMEM((seq, LANES), F32)]
                        * (tn // LANES * (0 if d == 1 else 1 if _stride_stages(d)[1] == 1 else 2))),
        compiler_params=_params("parallel", "arbitrary"),
        name=name,
    )(*args)


def _qkv_proj(xn, w_in, layer, tables, seq):
    tab_a, tab_b = tables[:2], tables[2:]
    wide = 2 * PROJ_TN
    n_a = 3 * A_WIDTH // PROJ_TN
    n_b = B_WIDTH // PROJ_TN
    assert n_b == B_GROUPS and A_WIDTH == wide
    proj = functools.partial(_proj_class, xn, w_in, layer, seq)
    a_qk = proj("proj_a_qk", (0, 1), wide, rope=True, tables=tab_a, scaled_tiles=1)
    v_plain = proj("proj_v_plain", (n_a - 2, n_a - 1, n_a + 2 * n_b), PROJ_TN)
    groups = []
    for gi, (_, d) in enumerate(B_PATTERNS):
        qk = proj(f"proj_b_qk_d{d}", (n_a + gi, n_a + n_b + gi), PROJ_TN,
                  rope=True, tables=tab_b, d=d)
        v = None if d == 1 else proj(f"proj_b_v_d{d}", (n_a + 2 * n_b + gi,), PROJ_TN, d=d)
        groups.append((qk, v))
    return a_qk, v_plain, groups


def _split_order(head_dim):
    half = head_dim // ROPE_FRAC // 2
    x1, x2, rest = [], [], []
    for base in range(0, LANES, head_dim):
        x1 += range(base, base + half)
        x2 += range(base + half, base + 2 * half)
        rest += range(base + 2 * half, base + head_dim)
    keep = LANES // 2 - len(x1)
    return x1 + rest[:keep] + x2 + rest[keep:]


def _lanes_of_first_head(lane, head_dim):
    order = _split_order(head_dim)
    mask, start = None, None
    for pos in range(LANES + 1):
        inside = pos < LANES and order[pos] < head_dim
        if inside and start is None:
            start = pos
        elif not inside and start is not None:
            run = (lane >= start) & (lane < pos)
            mask, start = run if mask is None else mask | run, None
    return mask


def _split_rope_tables(seq, head_dim):
    rot = head_dim // ROPE_FRAC
    heads = LANES // head_dim
    inv = ROPE_THETA ** (-(jnp.arange(0, rot, 2, dtype=F32) / rot))
    ang = jnp.arange(seq, dtype=F32)[:, None] * inv[None, :]
    cos, sin = jnp.tile(jnp.cos(ang), (1, heads)), jnp.tile(jnp.sin(ang), (1, heads))
    rest = LANES // 2 - cos.shape[1]
    one, zero = jnp.ones((seq, rest), F32), jnp.zeros((seq, rest), F32)
    return (jnp.concatenate([cos, one, cos, one], axis=-1),
            jnp.concatenate([-sin, zero, sin, zero], axis=-1))


def _w_in_kernel(w_ref, pa_ref, pb_ref, o_ref):
    j = pl.program_id(1)
    n_qa = 2 * A_WIDTH // PROJ_TN
    n_a = 3 * A_WIDTH // PROJ_TN
    n_b = B_WIDTH // PROJ_TN
    is_a = j < n_qa
    is_b = (j >= n_a) & (j < n_a + 2 * n_b)
    w = w_ref[...].astype(BF16)

    def reorder(p_ref):
        for m in range(PROJ_TN // MXU_WIDTH):
            cols = slice(m * MXU_WIDTH, (m + 1) * MXU_WIDTH)
            o_ref[:, cols] = jnp.dot(w[:, cols], p_ref[...],
                                     preferred_element_type=F32).astype(o_ref.dtype)

    pl.when(is_a)(lambda: reorder(pa_ref))
    pl.when(is_b)(lambda: reorder(pb_ref))

    @pl.when(jnp.logical_not(is_a | is_b))
    def _():
        o_ref[...] = w


def _reorder_matrix(head_dim):
    order = _split_order(head_dim)
    m = [[0.0] * MXU_WIDTH for _ in range(MXU_WIDTH)]
    for base in range(0, MXU_WIDTH, LANES):
        for new, old in enumerate(order):
            m[base + old][base + new] = 1.0
    return jnp.asarray(m, BF16)


def _prepare_w_in(w_in):
    depth, k, n = w_in.shape
    p_spec = pl.BlockSpec((MXU_WIDTH, MXU_WIDTH), lambda l, j: (0, 0))
    return pl.pallas_call(
        _w_in_kernel,
        grid=(depth, n // PROJ_TN),
        in_specs=[pl.BlockSpec((None, k, PROJ_TN), lambda l, j: (l, 0, j)), p_spec, p_spec],
        out_specs=pl.BlockSpec((None, k, PROJ_TN), lambda l, j: (l, 0, j)),
        out_shape=jax.ShapeDtypeStruct(w_in.shape, BF16),
        compiler_params=_params("parallel", "parallel"),
        name="w_in_prepare",
    )(w_in, _reorder_matrix(A_HEAD_DIM), _reorder_matrix(B_HEAD_DIM))


def _gate_kernel(x_ref, w_ref, b_ref, o_ref):
    x = x_ref[...]
    for m in range(PROJ_TN // MXU_WIDTH):
        cols = slice(m * MXU_WIDTH, (m + 1) * MXU_WIDTH)
        z = jnp.dot(x, w_ref[:, cols], preferred_element_type=F32) + b_ref[:, cols]
        o_ref[:, cols] = _sigmoid(z)


def _gate_proj(xn, w_in, bias, layer, seq):
    t = xn.shape[0]
    col0 = QKV_WIDTH // PROJ_TN
    return pl.pallas_call(
        _gate_kernel,
        grid=(t // seq, GATE_WIDTH // PROJ_TN),
        in_specs=[pl.BlockSpec((seq, D_MODEL), lambda i, j: (i, 0)),
                  pl.BlockSpec((None, D_MODEL, PROJ_TN), lambda i, j: (layer, 0, col0 + j)),
                  pl.BlockSpec((None, 1, PROJ_TN), lambda i, j: (layer, 0, j))],
        out_specs=pl.BlockSpec((seq, PROJ_TN), lambda i, j: (i, j)),
        out_shape=jax.ShapeDtypeStruct((t, GATE_WIDTH), F32),
        compiler_params=_params("parallel", "arbitrary"),
        name="gate_proj",
    )(xn, w_in, bias)


def _nt_dot(a, b):
    return lax.dot_general(a, b, (((1,), (1,)), ((), ())), preferred_element_type=F32)


def _diff_kernel(q_ref, k_ref, v_ref, lam_ref, g_ref, o_ref, vt_ref, *, seq, lam_init):
    lf = lam_ref[...]
    lam = (jnp.exp(jnp.sum(lf[0:1] * lf[1:2], axis=-1, keepdims=True))
           - jnp.exp(jnp.sum(lf[2:3] * lf[3:4], axis=-1, keepdims=True)) + lam_init)
    hd = 2 * A_HEAD_DIM
    k = k_ref[...]
    vt_ref[0:hd, :] = v_ref[...].astype(F32).T.astype(BF16)
    vt_ref[hd:, :] = jnp.ones((vt_ref.shape[0] - hd, seq), BF16)
    vt = vt_ref[...]
    g = g_ref[...] * (1.0 - lam_init)
    lane = lax.broadcasted_iota(jnp.int32, (DIFF_TQ, LANES), 1)
    first = _lanes_of_first_head(lane, A_HEAD_DIM)

    def scores(t):
        q = q_ref[t * DIFF_TQ:(t + 1) * DIFF_TQ, :]
        zero = jnp.zeros_like(q)
        return (_nt_dot(k, jnp.where(first, q, zero)),
                _nt_dot(k, jnp.where(first, zero, q)))

    nblk = seq // DIFF_TQ
    s_next = scores(0)
    for t in range(nblk):
        s1, s2 = s_next
        if t + 1 < nblk:
            s_next = scores(t + 1)
        e1 = jnp.exp(s1 - jnp.max(s1, axis=0, keepdims=True)).astype(BF16)
        e2 = jnp.exp(s2 - jnp.max(s2, axis=0, keepdims=True)).astype(BF16)
        u1 = jnp.dot(vt, e1, preferred_element_type=F32)
        u2 = jnp.dot(vt, e2, preferred_element_type=F32)
        o = u1[:hd] * (1.0 / u1[hd:hd + 1]) - u2[:hd] * (lam / u2[hd:hd + 1])
        ms = jnp.mean(o * o, axis=0, keepdims=True)
        o_ref[t * DIFF_TQ:(t + 1) * DIFF_TQ, :] = (
            (o * lax.rsqrt(ms + NORM_EPS) * g).T.astype(o_ref.dtype))


def _diff_attention(a_qk, v_plain, lam_all, subln_all, layer, seq):
    t = a_qk.shape[0]
    kb = A_WIDTH // LANES
    hd = 2 * A_HEAD_DIM
    blk = lambda off: pl.BlockSpec((seq, LANES), lambda b, h: (b, off + h))
    return pl.pallas_call(
        functools.partial(_diff_kernel, seq=seq, lam_init=0.8 - 0.6 * math.exp(-0.3 * layer)),
        grid=(t // seq, A_HEADS),
        in_specs=[blk(0), blk(kb), blk(0),
                  pl.BlockSpec((None, 4, A_HEAD_DIM), lambda b, h: (layer, 0, 0)),
                  pl.BlockSpec((None, hd, 1), lambda b, h: (layer, 0, 0))],
        out_specs=pl.BlockSpec((seq, LANES), lambda b, h: (b, h)),
        out_shape=jax.ShapeDtypeStruct((t, A_WIDTH), BF16),
        scratch_shapes=[pltpu.VMEM((hd + BF16_ROWS, seq), BF16)],
        compiler_params=_params("parallel", "parallel"),
        name="diff_attention",
    )(a_qk, a_qk, v_plain, lam_all, subln_all)


def _dil_kernel(*refs, seq):
    q_refs, k_refs, v_refs = refs[0:3], refs[3:6], refs[6:9]
    o_ref = refs[9]
    o_scr, l_scr = refs[10:13], refs[13:16]
    scale = B_HEAD_DIM ** -0.5
    row = lax.broadcasted_iota(jnp.int32, (DIL_TQ, DIL_WIN), 0)
    col = lax.broadcasted_iota(jnp.int32, (DIL_TQ, DIL_WIN), 1)
    rel = col - row

    def step(t, carry):
        q0 = pl.multiple_of(t * DIL_TQ, DIL_TQ)
        ws = pl.multiple_of(jnp.clip(q0 - DIL_HALF, 0, seq - DIL_WIN), DIL_HALF)
        band = jnp.abs(rel + (ws - q0)) <= DIL_HALF
        for gi, (window, d) in enumerate(B_PATTERNS):
            assert window // (2 * d) == DIL_HALF
            n_sub = seq // d
            nblk = n_sub // DIL_TQ
            run = t // nblk
            n = t % nblk
            lo = run * n_sub - ws
            p = _block_class(run, d)
            q = q_refs[gi][pl.ds(q0, DIL_TQ), :]
            k = k_refs[gi][pl.ds(ws, DIL_WIN), :]
            v = v_refs[gi][pl.ds(ws, DIL_WIN), :]
            s = _nt_dot(q, k) * scale
            ok = band & (col >= lo) & (col < lo + n_sub) if d > 1 else band
            s = jnp.where(ok, s, MASK_VALUE)
            m = jnp.max(s, axis=-1, keepdims=True)
            e = jnp.exp(s - m)
            l = jnp.sum(e, axis=-1, keepdims=True)
            o = jnp.dot(e.astype(v.dtype), v, preferred_element_type=F32) / l
            lse = jnp.broadcast_to(m + jnp.log(l), (DIL_TQ, LANES))
            dst = pl.ds(n * (DIL_TQ * d) + p, DIL_TQ, stride=d) if d > 1 else pl.ds(q0, DIL_TQ)
            o_scr[gi][dst, :] = o
            l_scr[gi][dst, :] = lse
        return carry

    lax.fori_loop(0, seq // DIL_TQ, step, 0, unroll=DIL_UNROLL)

    l0, l1, l2 = l_scr[0][...], l_scr[1][...], l_scr[2][...]
    m = jnp.maximum(jnp.maximum(l0, l1), l2)
    e0, e1, e2 = jnp.exp(l0 - m), jnp.exp(l1 - m), jnp.exp(l2 - m)
    num = e0 * o_scr[0][...] + e1 * o_scr[1][...] + e2 * o_scr[2][...]
    o_ref[...] = (num / (e0 + e1 + e2)).astype(o_ref.dtype)


def _dil_attention(v_plain, groups, seq):
    t = v_plain.shape[0]
    blk = lambda off: pl.BlockSpec((seq, LANES), lambda b, h: (b, off + h))
    qs = [qk for qk, _ in groups]
    vs = [v_plain if v is None else v for _, v in groups]
    v_off = [A_WIDTH // LANES if v is None else 0 for _, v in groups]
    in_specs = ([blk(0)] * B_GROUPS + [blk(B_HEADS)] * B_GROUPS + [blk(off) for off in v_off])
    return pl.pallas_call(
        functools.partial(_dil_kernel, seq=seq),
        grid=(t // seq, B_HEADS),
        in_specs=in_specs,
        out_specs=pl.BlockSpec((seq, LANES), lambda b, h: (b, h)),
        out_shape=jax.ShapeDtypeStruct((t, B_OUT), BF16),
        scratch_shapes=[pltpu.VMEM((seq, LANES), F32)] * (2 * B_GROUPS),
        compiler_params=_params("parallel", "parallel"),
        name="dilated_attention",
    )(*qs, *qs, *vs)


def _mix_kernel(oa_ref, ob_ref, ga_ref, gb_ref, x_ref, wpa_ref, wpb_ref, wo_ref, g_ref,
                xo_ref, hn_ref):
    ya = jnp.dot(oa_ref[...], wpa_ref[...], preferred_element_type=F32)
    yb = jnp.dot(ob_ref[...], wpb_ref[...], preferred_element_type=F32)
    merged = (ga_ref[...] * ya + gb_ref[...] * yb).astype(BF16)
    xo = x_ref[...] + jnp.dot(merged, wo_ref[...], preferred_element_type=F32)
    xo_ref[...] = xo
    hn_ref[...] = _rms(xo, g_ref[...]).astype(hn_ref.dtype)


def _mix_out(oa, ob, gates, x, w_pa, w_pb, w_out, norm_ffn, layer):
    t = x.shape[0]
    row = lambda width, col=0: pl.BlockSpec((MIX_TM, width), lambda i: (i, col))
    whole = lambda k, n: pl.BlockSpec((None, k, n), lambda i: (layer, 0, 0),
                                      pipeline_mode=pl.Buffered(1))
    return pl.pallas_call(
        _mix_kernel,
        grid=(t // MIX_TM,),
        in_specs=[row(A_WIDTH), row(B_OUT), row(D_MODEL, 0), row(D_MODEL, 1), row(D_MODEL),
                  whole(A_WIDTH, D_MODEL), whole(B_OUT, D_MODEL), whole(D_MODEL, D_MODEL),
                  pl.BlockSpec((None, 1, D_MODEL), lambda i: (layer, 0, 0))],
        out_specs=[row(D_MODEL), row(D_MODEL)],
        out_shape=[jax.ShapeDtypeStruct((t, D_MODEL), F32),
                   jax.ShapeDtypeStruct((t, D_MODEL), BF16)],
        compiler_params=_params("parallel"),
        name="mix_out",
    )(oa, ob, gates, gates, x, w_pa, w_pb, w_out, norm_ffn)


def _ffn_kernel(hn_ref, wg_ref, wu_ref, w2_ref, x_ref, g_ref, *outs):
    acc_ref = outs[0]
    k = pl.program_id(1)

    @pl.when(k == 0)
    def _():
        acc_ref[...] = x_ref[...]

    hn = hn_ref[...]
    width = FFN_TF // FFN_CHUNKS
    acts = []
    for c in range(FFN_CHUNKS):
        cols = slice(c * width, (c + 1) * width)
        hg = jnp.dot(hn, wg_ref[:, cols], preferred_element_type=F32)
        hu = jnp.dot(hn, wu_ref[:, cols], preferred_element_type=F32)
        acts.append((hg * _sigmoid(hg) * hu).astype(BF16))
    part = None
    for c in range(FFN_CHUNKS):
        rows = slice(c * width, (c + 1) * width)
        p = jnp.dot(acts[c], w2_ref[rows, :], preferred_element_type=F32)
        part = p if part is None else part + p
    acc_ref[...] += part

    @pl.when(k == pl.num_programs(1) - 1)
    def _():
        outs[-1][...] = _rms(acc_ref[...], g_ref[...]).astype(outs[-1].dtype)


def _ffn(hn, w_ffn_in, w_ffn_out, x, norm_all, layer, norm_layer, emit_x, tm, tile0=0, n_tiles=None):
    if n_tiles is None:
        n_tiles = x.shape[0] // tm
    t = n_tiles * tm
    nk = D_FF // FFN_TF
    row_in = pl.BlockSpec((tm, D_MODEL), lambda i, k: (tile0 + i, 0))
    row_out = pl.BlockSpec((tm, D_MODEL), lambda i, k: (i, 0))
    if norm_all.ndim == 3:
        g_spec = pl.BlockSpec((None, 1, D_MODEL), lambda i, k: (norm_layer, 0, 0))
    else:
        g_spec = pl.BlockSpec((1, D_MODEL), lambda i, k: (0, 0))
    out_specs = [row_out]
    out_shape = [jax.ShapeDtypeStruct((t, D_MODEL), F32)]
    if emit_x:
        out_specs = [row_out, row_out]
        out_shape = out_shape + [jax.ShapeDtypeStruct((t, D_MODEL), BF16)]
    return pl.pallas_call(
        _ffn_kernel,
        grid=(n_tiles, nk),
        in_specs=[row_in,
                  pl.BlockSpec((None, D_MODEL, FFN_TF), lambda i, k: (layer, 0, k)),
                  pl.BlockSpec((None, D_MODEL, FFN_TF), lambda i, k: (layer, 0, nk + k)),
                  pl.BlockSpec((None, FFN_TF, D_MODEL), lambda i, k: (layer, k, 0)),
                  row_in, g_spec],
        out_specs=out_specs,
        out_shape=out_shape,
        compiler_params=_params("parallel", "arbitrary"),
        name="ffn",
    )(hn, w_ffn_in, w_ffn_in, w_ffn_out, x, norm_all)


def _trunk(x_parts, seq, norm_mix, norm_ffn, w_in, gate_bias, diff_lambda, diff_subln,
           w_pa, w_pb, w_out, w_ffn_in, w_ffn_out, norm_final):
    tables = _split_rope_tables(seq, A_HEAD_DIM) + _split_rope_tables(seq, B_HEAD_DIM)
    x, xn = _rmsnorm(x_parts, norm_mix, 0)
    for layer in range(DEPTH):
        a_qk, v_plain, groups = _qkv_proj(xn, w_in, layer, tables, seq)
        gates = _gate_proj(xn, w_in, gate_bias, layer, seq)
        oa = _diff_attention(a_qk, v_plain, diff_lambda, diff_subln, layer, seq)
        ob = _dil_attention(v_plain, groups, seq)
        x, hn = _mix_out(oa, ob, gates, x, w_pa, w_pb, w_out, norm_ffn, layer)
        if layer + 1 < DEPTH:
            tm = FFN_TM_WIDE if x.shape[0] % FFN_TM_WIDE == 0 else FFN_TM
            x, xn = _ffn(hn, w_ffn_in, w_ffn_out, x, norm_mix, layer, layer + 1, True, tm)
    outs, tile0 = [], 0
    for part in x_parts:
        n_tiles = part.shape[0] // FFN_TM
        (y,) = _ffn(hn, w_ffn_in, w_ffn_out, x, norm_final, DEPTH - 1, 0, False, FFN_TM,
                    tile0, n_tiles)
        outs.append(y)
        tile0 += n_tiles
    return outs


def kernel(x_prompt, x_sample, norm_mix, norm_ffn, w_in, gate_bias, diff_lambda, diff_subln,
           w_proj_a, w_proj_b, w_out, w_ffn_in, w_ffn_out, norm_final):
    bp, seq, d = x_prompt.shape
    bs = x_sample.shape[0]
    assert x_sample.shape[1:] == (seq, d) and d == D_MODEL
    assert seq % (B_PATTERNS[-1][1] * DIL_TQ) == 0 and seq >= DIL_WIN
    y_prompt, y_sample = _trunk(
        [x_prompt.reshape(bp * seq, d), x_sample.reshape(bs * seq, d)], seq,
        norm_mix.reshape(DEPTH, 1, D_MODEL), norm_ffn.reshape(DEPTH, 1, D_MODEL),
        _prepare_w_in(w_in), gate_bias.reshape(DEPTH, 1, GATE_WIDTH),
        diff_lambda, diff_subln.reshape(DEPTH, 2 * A_HEAD_DIM, 1),
        w_proj_a.astype(BF16), w_proj_b.astype(BF16), w_out.astype(BF16),
        w_ffn_in.astype(BF16), w_ffn_out.astype(BF16), norm_final.reshape(1, D_MODEL))
    return (y_prompt.reshape(bp, seq, d), y_sample.reshape(bs, seq, d))
```

```python
import functools
import math

import jax
import jax.numpy as jnp
from jax import lax
from jax.experimental import pallas as pl
from jax.experimental.pallas import tpu as pltpu

D_MODEL = 2048
DEPTH = 4
A_HEADS = 8
A_HEAD_DIM = 64
A_WIDTH = A_HEADS * 2 * A_HEAD_DIM
B_PATTERNS = ((128, 1), (512, 4), (2048, 16))
B_GROUPS = len(B_PATTERNS)
B_HEADS = 4
B_HEAD_DIM = 128
B_WIDTH = B_GROUPS * B_HEADS * B_HEAD_DIM
B_OUT = B_HEADS * B_HEAD_DIM
QKV_WIDTH = 3 * A_WIDTH + 3 * B_WIDTH
GATE_WIDTH = 2 * D_MODEL
D_FF = 5632
ROPE_THETA = 500000.0
ROPE_FRAC = 4
NORM_EPS = 1e-6
MASK_VALUE = -1e30
LOG2_E = 1.4426950408889634

LANES = 128
BF16_ROWS = 16
MXU_WIDTH = 256
VMEM_LIMIT = 60 * 1024 * 1024

PROJ_TN = 512
GATE_TILES = 2
DIFF_TQ = 512
DIFF_HEADS = 2
DIL_UNROLL = 4
DIL_TQ = 128
DIL_HALF = 64
DIL_WIN = 256
MAX_ROW_STRIDE = 4
MIX_TM = 512
FFN_TM = 512
FFN_TM_WIDE = 768
FFN_TF = 512
FFN_CHUNKS = 2
NORM_TM = 512

BF16 = jnp.bfloat16
F32 = jnp.float32


def _params(*sem):
    return pltpu.CompilerParams(dimension_semantics=sem, vmem_limit_bytes=VMEM_LIMIT)


def _sigmoid(x):
    return 0.5 * jnp.tanh(0.5 * x) + 0.5


def _rms(x, g):
    ms = jnp.mean(x * x, axis=-1, keepdims=True)
    return x * lax.rsqrt(ms + NORM_EPS) * g


def _stream_specs(parts, tm):
    specs, firsts, start = [], [], 0
    for arr in parts:
        n = arr.shape[0] // tm
        specs.append(pl.BlockSpec((tm, D_MODEL),
                                  lambda i, *_, start=start, n=n: (jnp.clip(i - start, 0, n - 1), 0)))
        firsts.append(start)
        start += n
    return specs, tuple(firsts)


def _stream_tile(refs, firsts, i):
    x = refs[0][...]
    for ref, first in zip(refs[1:], firsts[1:]):
        x = jnp.where(i >= first, ref[...], x)
    return x


def _rmsnorm_kernel(*refs, firsts):
    x_refs, (g_ref, x_ref, o_ref) = refs[:-3], refs[-3:]
    x = _stream_tile(x_refs, firsts, pl.program_id(0))
    x_ref[...] = x
    o_ref[...] = _rms(x, g_ref[...]).astype(o_ref.dtype)


def _rmsnorm(parts, g_all, layer):
    t = sum(p.shape[0] for p in parts)
    specs, firsts = _stream_specs(parts, NORM_TM)
    row = pl.BlockSpec((NORM_TM, D_MODEL), lambda i: (i, 0))
    return pl.pallas_call(
        functools.partial(_rmsnorm_kernel, firsts=firsts),
        grid=(t // NORM_TM,),
        in_specs=specs + [pl.BlockSpec((None, 1, D_MODEL), lambda i: (layer, 0, 0))],
        out_specs=[row, row],
        out_shape=[jax.ShapeDtypeStruct((t, D_MODEL), F32), jax.ShapeDtypeStruct((t, D_MODEL), BF16)],
        compiler_params=_params("parallel"),
        name="rmsnorm_in",
    )(*parts, g_all)


def _stride_stages(d):
    if d <= MAX_ROW_STRIDE:
        return d, 1
    assert d % MAX_ROW_STRIDE == 0 and d // MAX_ROW_STRIDE <= MAX_ROW_STRIDE
    return MAX_ROW_STRIDE, d // MAX_ROW_STRIDE


def _block_class(blk, d):
    s1, s2 = _stride_stages(d)
    return blk if s2 == 1 else s1 * (blk % s2) + blk // s2


def _proj_kernel(x_ref, w_ref, *rest, seq, rope, d, tn, scaled_tiles):
    if rope:
        c_ref, s_ref, o_ref, *scr = rest
    else:
        o_ref, *scr = rest
    j = pl.program_id(1)
    n_sub = seq // d
    x = x_ref[...]
    assert MXU_WIDTH == 2 * LANES
    for m in range(tn // MXU_WIDTH):
        acc = jnp.dot(x, w_ref[:, m * MXU_WIDTH:(m + 1) * MXU_WIDTH], preferred_element_type=F32)
        halves = [acc[:, :LANES], acc[:, LANES:]]
        if rope:
            halves = [h * c_ref[...] + pltpu.roll(h, LANES // 2, 1) * s_ref[...] for h in halves]
        for c, val in enumerate(halves):
            chunk = m * (MXU_WIDTH // LANES) + c
            sl = slice(chunk * LANES, (chunk + 1) * LANES)
            if scaled_tiles:
                val = val * jnp.where(j < scaled_tiles, A_HEAD_DIM ** -0.5, 1.0).astype(F32)
            if d == 1:
                o_ref[:, sl] = val.astype(o_ref.dtype)
                continue
            buf = scr[chunk]
            buf[...] = val
            s1, s2 = _stride_stages(d)
            if s2 > 1:
                tmp, n1 = scr[tn // LANES + chunk], seq // s1
                for p1 in range(s1):
                    tmp[p1 * n1:(p1 + 1) * n1, :] = buf[pl.ds(p1, n1, stride=s1), :]
                runs = [(tmp, p1 * n1, s2) for p1 in range(s1)]
            else:
                runs = [(buf, 0, s1)]
            blk = 0
            for src, base, stride in runs:
                for p in range(stride):
                    o_ref[blk * n_sub:(blk + 1) * n_sub, sl] = (
                        src[pl.ds(base + p, n_sub, stride=stride), :].astype(o_ref.dtype))
                    blk += 1


def _proj_class(xn, w_in, layer, seq, name, tiles, tn, rope=None, tables=None, d=1, scaled_tiles=0):
    t = xn.shape[0]

    def w_map(i, j):
        tile = tiles[0]
        for idx in range(1, len(tiles)):
            tile = jnp.where(j >= idx, tiles[idx], tile)
        return (layer, 0, tile)

    in_specs = [pl.BlockSpec((seq, D_MODEL), lambda i, j: (i, 0)),
                pl.BlockSpec((None, D_MODEL, tn), w_map)]
    args = [xn, w_in]
    if rope:
        in_specs += [pl.BlockSpec((seq, LANES), lambda i, j: (0, 0))] * 2
        args += list(tables)
    return pl.pallas_call(
        functools.partial(_proj_kernel, seq=seq, rope=rope, d=d, tn=tn, scaled_tiles=scaled_tiles),
        grid=(t // seq, len(tiles)),
        in_specs=in_specs,
        out_specs=pl.BlockSpec((seq, tn), lambda i, j: (i, j)),
        out_shape=jax.ShapeDtypeStruct((t, len(tiles) * tn), BF16),
        scratch_shapes=([pltpu.VMEM((seq, LANES), F32)]
                        * (tn // LANES * (0 if d == 1 else 1 if _stride_stages(d)[1] == 1 else 2))),
        compiler_params=_params("parallel", "arbitrary"),
        name=name,
    )(*args)


def _qkv_proj(xn, w_in, layer, tables, seq):
    tab_a, tab_b = tables[:2], tables[2:]
    wide = 2 * PROJ_TN
    n_a = 3 * A_WIDTH // PROJ_TN
    n_b = B_WIDTH // PROJ_TN
    assert n_b == B_GROUPS and A_WIDTH == wide
    proj = functools.partial(_proj_class, xn, w_in, layer, seq)
    a_qk = proj("proj_a_qk", (0, 1), wide, rope=True, tables=tab_a, scaled_tiles=1)
    v_plain = proj("proj_v_plain", (n_a - 2, n_a - 1, n_a + 2 * n_b), PROJ_TN)
    groups = []
    for gi, (_, d) in enumerate(B_PATTERNS):
        qk = proj(f"proj_b_qk_d{d}", (n_a + gi, n_a + n_b + gi), PROJ_TN,
                  rope=True, tables=tab_b, d=d)
        v = None if d == 1 else proj(f"proj_b_v_d{d}", (n_a + 2 * n_b + gi,), PROJ_TN, d=d)
        groups.append((qk, v))
    return a_qk, v_plain, groups


def _split_order(head_dim):
    half = head_dim // ROPE_FRAC // 2
    x1, x2, rest = [], [], []
    for base in range(0, LANES, head_dim):
        x1 += range(base, base + half)
        x2 += range(base + half, base + 2 * half)
        rest += range(base + 2 * half, base + head_dim)
    keep = LANES // 2 - len(x1)
    return x1 + rest[:keep] + x2 + rest[keep:]


def _lanes_of_first_head(lane, head_dim):
    order = _split_order(head_dim)
    mask, start = None, None
    for pos in range(LANES + 1):
        inside = pos < LANES and order[pos] < head_dim
        if inside and start is None:
            start = pos
        elif not inside and start is not None:
            run = (lane >= start) & (lane < pos)
            mask, start = run if mask is None else mask | run, None
    return mask


def _split_rope_tables(seq, head_dim):
    rot = head_dim // ROPE_FRAC
    heads = LANES // head_dim
    inv = ROPE_THETA ** (-(jnp.arange(0, rot, 2, dtype=F32) / rot))
    ang = jnp.arange(seq, dtype=F32)[:, None] * inv[None, :]
    cos, sin = jnp.tile(jnp.cos(ang), (1, heads)), jnp.tile(jnp.sin(ang), (1, heads))
    rest = LANES // 2 - cos.shape[1]
    one, zero = jnp.ones((seq, rest), F32), jnp.zeros((seq, rest), F32)
    return (jnp.concatenate([cos, one, cos, one], axis=-1),
            jnp.concatenate([-sin, zero, sin, zero], axis=-1))


def _w_in_kernel(w_ref, pa_ref, pb_ref, o_ref):
    j = pl.program_id(1)
    n_qa = 2 * A_WIDTH // PROJ_TN
    n_a = 3 * A_WIDTH // PROJ_TN
    n_b = B_WIDTH // PROJ_TN
    is_a = j < n_qa
    is_b = (j >= n_a) & (j < n_a + 2 * n_b)
    w = w_ref[...].astype(BF16)

    def reorder(p_ref):
        for m in range(PROJ_TN // MXU_WIDTH):
            cols = slice(m * MXU_WIDTH, (m + 1) * MXU_WIDTH)
            o_ref[:, cols] = jnp.dot(w[:, cols], p_ref[...],
                                     preferred_element_type=F32).astype(o_ref.dtype)

    pl.when(is_a)(lambda: reorder(pa_ref))
    pl.when(is_b)(lambda: reorder(pb_ref))

    @pl.when(jnp.logical_not(is_a | is_b))
    def _():
        o_ref[...] = w


def _reorder_matrix(head_dim):
    order = _split_order(head_dim)
    m = [[0.0] * MXU_WIDTH for _ in range(MXU_WIDTH)]
    for base in range(0, MXU_WIDTH, LANES):
        for new, old in enumerate(order):
            m[base + old][base + new] = 1.0
    return jnp.asarray(m, BF16)


def _prepare_w_in(w_in):
    depth, k, n = w_in.shape
    p_spec = pl.BlockSpec((MXU_WIDTH, MXU_WIDTH), lambda l, j: (0, 0))
    return pl.pallas_call(
        _w_in_kernel,
        grid=(depth, n // PROJ_TN),
        in_specs=[pl.BlockSpec((None, k, PROJ_TN), lambda l, j: (l, 0, j)), p_spec, p_spec],
        out_specs=pl.BlockSpec((None, k, PROJ_TN), lambda l, j: (l, 0, j)),
        out_shape=jax.ShapeDtypeStruct(w_in.shape, BF16),
        compiler_params=_params("parallel", "parallel"),
        name="w_in_prepare",
    )(w_in, _reorder_matrix(A_HEAD_DIM), _reorder_matrix(B_HEAD_DIM))


def _gate_kernel(x_ref, *refs):
    w_refs, (b_ref, o_ref) = refs[:-2], refs[-2:]
    x = x_ref[...]
    for n, w_ref in enumerate(w_refs):
        for m in range(PROJ_TN // MXU_WIDTH):
            cols = slice(m * MXU_WIDTH, (m + 1) * MXU_WIDTH)
            out_cols = slice(n * PROJ_TN + cols.start, n * PROJ_TN + cols.stop)
            z = jnp.dot(x, w_ref[:, cols], preferred_element_type=F32) + b_ref[:, out_cols]
            o_ref[:, out_cols] = _sigmoid(z)


def _gate_proj(xn, w_in, bias, layer, seq):
    t = xn.shape[0]
    col0 = QKV_WIDTH // PROJ_TN
    wide = GATE_TILES * PROJ_TN
    w_spec = lambda n: pl.BlockSpec((None, D_MODEL, PROJ_TN),
                                    lambda i, j: (layer, 0, col0 + GATE_TILES * j + n))
    return pl.pallas_call(
        _gate_kernel,
        grid=(t // seq, GATE_WIDTH // wide),
        in_specs=[pl.BlockSpec((seq, D_MODEL), lambda i, j: (i, 0))]
        + [w_spec(n) for n in range(GATE_TILES)]
        + [pl.BlockSpec((None, 1, wide), lambda i, j: (layer, 0, j))],
        out_specs=pl.BlockSpec((seq, wide), lambda i, j: (i, j)),
        out_shape=jax.ShapeDtypeStruct((t, GATE_WIDTH), F32),
        compiler_params=_params("parallel", "arbitrary"),
        name="gate_proj",
    )(xn, *([w_in] * GATE_TILES), bias)


def _nt_dot(a, b):
    return lax.dot_general(a, b, (((1,), (1,)), ((), ())), preferred_element_type=F32)


def _diff_kernel(q_ref, k_ref, v_ref, lam_ref, g_ref, o_ref, *vt_refs, seq, lam_init):
    lf = lam_ref[...]
    lam = (jnp.exp(jnp.sum(lf[0:1] * lf[1:2], axis=-1, keepdims=True))
           - jnp.exp(jnp.sum(lf[2:3] * lf[3:4], axis=-1, keepdims=True)) + lam_init)
    hd = 2 * A_HEAD_DIM
    head_cols = [slice(a * LANES, (a + 1) * LANES) for a in range(len(vt_refs))]
    for vt_ref, cols in zip(vt_refs, head_cols):
        vt_ref[0:hd, :] = v_ref[:, cols].astype(F32).T.astype(BF16)
        vt_ref[hd:, :] = jnp.ones((vt_ref.shape[0] - hd, seq), BF16)
    g = g_ref[...] * (1.0 - lam_init)
    lane = lax.broadcasted_iota(jnp.int32, (DIFF_TQ, LANES), 1)
    first = _lanes_of_first_head(lane, A_HEAD_DIM)

    def scores(a, t):
        q = q_ref[t * DIFF_TQ:(t + 1) * DIFF_TQ, head_cols[a]]
        k = k_ref[:, head_cols[a]]
        zero = jnp.zeros_like(q)
        return (_nt_dot(k, jnp.where(first, q, zero)),
                _nt_dot(k, jnp.where(first, zero, q)))

    items = [(a, t) for a in range(len(vt_refs)) for t in range(seq // DIFF_TQ)]
    s_next = scores(*items[0])
    for i, (a, t) in enumerate(items):
        s1, s2 = s_next
        if i + 1 < len(items):
            s_next = scores(*items[i + 1])
        e1 = jnp.exp(s1 - jnp.max(s1, axis=0, keepdims=True)).astype(BF16)
        e2 = jnp.exp(s2 - jnp.max(s2, axis=0, keepdims=True)).astype(BF16)
        vt = vt_refs[a][...]
        u1 = jnp.dot(vt, e1, preferred_element_type=F32)
        u2 = jnp.dot(vt, e2, preferred_element_type=F32)
        o = u1[:hd] * (1.0 / u1[hd:hd + 1]) - u2[:hd] * (lam / u2[hd:hd + 1])
        ms = jnp.mean(o * o, axis=0, keepdims=True)
        o_ref[t * DIFF_TQ:(t + 1) * DIFF_TQ, head_cols[a]] = (
            (o * lax.rsqrt(ms + NORM_EPS) * g).T.astype(o_ref.dtype))


def _diff_attention(a_qk, v_plain, lam_all, subln_all, layer, seq):
    t = a_qk.shape[0]
    hd = 2 * A_HEAD_DIM
    width = DIFF_HEADS * LANES
    k_off = A_WIDTH // width
    blk = lambda off: pl.BlockSpec((seq, width), lambda b, h: (b, off + h))
    return pl.pallas_call(
        functools.partial(_diff_kernel, seq=seq, lam_init=0.8 - 0.6 * math.exp(-0.3 * layer)),
        grid=(t // seq, A_HEADS // DIFF_HEADS),
        in_specs=[blk(0), blk(k_off), blk(0),
                  pl.BlockSpec((None, 4, A_HEAD_DIM), lambda b, h: (layer, 0, 0)),
                  pl.BlockSpec((None, hd, 1), lambda b, h: (layer, 0, 0))],
        out_specs=blk(0),
        out_shape=jax.ShapeDtypeStruct((t, A_WIDTH), BF16),
        scratch_shapes=[pltpu.VMEM((hd + BF16_ROWS, seq), BF16)] * DIFF_HEADS,
        compiler_params=_params("parallel", "parallel"),
        name="diff_attention",
    )(a_qk, a_qk, v_plain, lam_all, subln_all)


def _dil_kernel(*refs, seq):
    q_refs, k_refs, v_refs = refs[0:3], refs[3:6], refs[6:9]
    o_ref = refs[9]
    o_scr, l_scr = refs[10:13], refs[13:16]
    scale = B_HEAD_DIM ** -0.5
    row = lax.broadcasted_iota(jnp.int32, (DIL_TQ, DIL_WIN), 0)
    col = lax.broadcasted_iota(jnp.int32, (DIL_TQ, DIL_WIN), 1)
    rel = col - row
    col1 = lax.broadcasted_iota(jnp.int32, (1, DIL_WIN), 1)

    def step(t, carry):
        q0 = pl.multiple_of(t * DIL_TQ, DIL_TQ)
        ws = pl.multiple_of(jnp.clip(q0 - DIL_HALF, 0, seq - DIL_WIN), DIL_HALF)
        band = jnp.abs(rel + (ws - q0)) <= DIL_HALF
        for gi, (window, d) in enumerate(B_PATTERNS):
            assert window // (2 * d) == DIL_HALF
            n_sub = seq // d
            nblk = n_sub // DIL_TQ
            run = t // nblk
            n = t % nblk
            lo = run * n_sub - ws
            p = _block_class(run, d)
            q = q_refs[gi][pl.ds(q0, DIL_TQ), :]
            k = k_refs[gi][pl.ds(ws, DIL_WIN), :]
            v = v_refs[gi][pl.ds(ws, DIL_WIN), :]
            s = jnp.where(band, _nt_dot(q, k), MASK_VALUE)
            if d > 1:
                s = s + jnp.where((col1 >= lo) & (col1 < lo + n_sub), 0.0, MASK_VALUE)
            m = jnp.max(s, axis=-1, keepdims=True)
            e = jnp.exp2((s - m) * (scale * LOG2_E))
            l = jnp.sum(e, axis=-1, keepdims=True)
            o = jnp.dot(e.astype(v.dtype), v, preferred_element_type=F32) / l
            lse = jnp.broadcast_to(m * scale + jnp.log(l), (DIL_TQ, LANES))
            dst = pl.ds(n * (DIL_TQ * d) + p, DIL_TQ, stride=d) if d > 1 else pl.ds(q0, DIL_TQ)
            o_scr[gi][dst, :] = o
            l_scr[gi][dst, :] = lse
        return carry

    lax.fori_loop(0, seq // DIL_TQ, step, 0, unroll=DIL_UNROLL)

    l0, l1, l2 = l_scr[0][...], l_scr[1][...], l_scr[2][...]
    m = jnp.maximum(jnp.maximum(l0, l1), l2)
    e0, e1, e2 = jnp.exp(l0 - m), jnp.exp(l1 - m), jnp.exp(l2 - m)
    num = e0 * o_scr[0][...] + e1 * o_scr[1][...] + e2 * o_scr[2][...]
    o_ref[...] = (num / (e0 + e1 + e2)).astype(o_ref.dtype)


def _dil_attention(v_plain, groups, seq):
    t = v_plain.shape[0]
    blk = lambda off: pl.BlockSpec((seq, LANES), lambda b, h: (b, off + h))
    qs = [qk for qk, _ in groups]
    vs = [v_plain if v is None else v for _, v in groups]
    v_off = [A_WIDTH // LANES if v is None else 0 for _, v in groups]
    in_specs = ([blk(0)] * B_GROUPS + [blk(B_HEADS)] * B_GROUPS + [blk(off) for off in v_off])
    return pl.pallas_call(
        functools.partial(_dil_kernel, seq=seq),
        grid=(t // seq, B_HEADS),
        in_specs=in_specs,
        out_specs=pl.BlockSpec((seq, LANES), lambda b, h: (b, h)),
        out_shape=jax.ShapeDtypeStruct((t, B_OUT), BF16),
        scratch_shapes=[pltpu.VMEM((seq, LANES), F32)] * (2 * B_GROUPS),
        compiler_params=_params("parallel", "parallel"),
        name="dilated_attention",
    )(*qs, *qs, *vs)


def _mix_kernel(oa_ref, ob_ref, ga_ref, gb_ref, x_ref, wpa_ref, wpb_ref, wo_ref, g_ref,
                xo_ref, hn_ref):
    ya = jnp.dot(oa_ref[...], wpa_ref[...], preferred_element_type=F32)
    yb = jnp.dot(ob_ref[...], wpb_ref[...], preferred_element_type=F32)
    merged = (ga_ref[...] * ya + gb_ref[...] * yb).astype(BF16)
    xo = x_ref[...] + jnp.dot(merged, wo_ref[...], preferred_element_type=F32)
    xo_ref[...] = xo
    hn_ref[...] = _rms(xo, g_ref[...]).astype(hn_ref.dtype)


def _mix_out(oa, ob, gates, x, w_pa, w_pb, w_out, norm_ffn, layer):
    t = x.shape[0]
    row = lambda width, col=0: pl.BlockSpec((MIX_TM, width), lambda i: (i, col))
    whole = lambda k, n: pl.BlockSpec((None, k, n), lambda i: (layer, 0, 0),
                                      pipeline_mode=pl.Buffered(1))
    return pl.pallas_call(
        _mix_kernel,
        grid=(t // MIX_TM,),
        in_specs=[row(A_WIDTH), row(B_OUT), row(D_MODEL, 0), row(D_MODEL, 1), row(D_MODEL),
                  whole(A_WIDTH, D_MODEL), whole(B_OUT, D_MODEL), whole(D_MODEL, D_MODEL),
                  pl.BlockSpec((None, 1, D_MODEL), lambda i: (layer, 0, 0))],
        out_specs=[row(D_MODEL), row(D_MODEL)],
        out_shape=[jax.ShapeDtypeStruct((t, D_MODEL), F32),
                   jax.ShapeDtypeStruct((t, D_MODEL), BF16)],
        compiler_params=_params("parallel"),
        name="mix_out",
    )(oa, ob, gates, gates, x, w_pa, w_pb, w_out, norm_ffn)


def _ffn_kernel(hn_ref, wg_ref, wu_ref, w2_ref, x_ref, g_ref, *outs):
    acc_ref = outs[0]
    k = pl.program_id(1)

    @pl.when(k == 0)
    def _():
        acc_ref[...] = x_ref[...]

    hn = hn_ref[...]
    width = FFN_TF // FFN_CHUNKS
    acts = []
    for c in range(FFN_CHUNKS):
        cols = slice(c * width, (c + 1) * width)
        hg = jnp.dot(hn, wg_ref[:, cols], preferred_element_type=F32)
        hu = jnp.dot(hn, wu_ref[:, cols], preferred_element_type=F32)
        acts.append((hg * _sigmoid(hg) * hu).astype(BF16))
    part = None
    for c in range(FFN_CHUNKS):
        rows = slice(c * width, (c + 1) * width)
        p = jnp.dot(acts[c], w2_ref[rows, :], preferred_element_type=F32)
        part = p if part is None else part + p
    acc_ref[...] += part

    @pl.when(k == pl.num_programs(1) - 1)
    def _():
        outs[-1][...] = _rms(acc_ref[...], g_ref[...]).astype(outs[-1].dtype)


def _ffn(hn, w_ffn_in, w_ffn_out, x, norm_all, layer, norm_layer, emit_x, tm, tile0=0, n_tiles=None):
    if n_tiles is None:
        n_tiles = x.shape[0] // tm
    t = n_tiles * tm
    nk = D_FF // FFN_TF
    row_in = pl.BlockSpec((tm, D_MODEL), lambda i, k: (tile0 + i, 0))
    row_out = pl.BlockSpec((tm, D_MODEL), lambda i, k: (i, 0))
    if norm_all.ndim == 3:
        g_spec = pl.BlockSpec((None, 1, D_MODEL), lambda i, k: (norm_layer, 0, 0))
    else:
        g_spec = pl.BlockSpec((1, D_MODEL), lambda i, k: (0, 0))
    out_specs = [row_out]
    out_shape = [jax.ShapeDtypeStruct((t, D_MODEL), F32)]
    if emit_x:
        out_specs = [row_out, row_out]
        out_shape = out_shape + [jax.ShapeDtypeStruct((t, D_MODEL), BF16)]
    return pl.pallas_call(
        _ffn_kernel,
        grid=(n_tiles, nk),
        in_specs=[row_in,
                  pl.BlockSpec((None, D_MODEL, FFN_TF), lambda i, k: (layer, 0, k)),
                  pl.BlockSpec((None, D_MODEL, FFN_TF), lambda i, k: (layer, 0, nk + k)),
                  pl.BlockSpec((None, FFN_TF, D_MODEL), lambda i, k: (layer, k, 0)),
                  row_in, g_spec],
        out_specs=out_specs,
        out_shape=out_shape,
        compiler_params=_params("parallel", "arbitrary"),
        name="ffn",
    )(hn, w_ffn_in, w_ffn_in, w_ffn_out, x, norm_all)


def _trunk(x_parts, seq, norm_mix, norm_ffn, w_in, gate_bias, diff_lambda, diff_subln,
           w_pa, w_pb, w_out, w_ffn_in, w_ffn_out, norm_final):
    tables = _split_rope_tables(seq, A_HEAD_DIM) + _split_rope_tables(seq, B_HEAD_DIM)
    x, xn = _rmsnorm(x_parts, norm_mix, 0)
    for layer in range(DEPTH):
        a_qk, v_plain, groups = _qkv_proj(xn, w_in, layer, tables, seq)
        gates = _gate_proj(xn, w_in, gate_bias, layer, seq)
        oa = _diff_attention(a_qk, v_plain, diff_lambda, diff_subln, layer, seq)
        ob = _dil_attention(v_plain, groups, seq)
        x, hn = _mix_out(oa, ob, gates, x, w_pa, w_pb, w_out, norm_ffn, layer)
        if layer + 1 < DEPTH:
            tm = FFN_TM_WIDE if x.shape[0] % FFN_TM_WIDE == 0 else FFN_TM
            x, xn = _ffn(hn, w_ffn_in, w_ffn_out, x, norm_mix, layer, layer + 1, True, tm)
    outs, tile0 = [], 0
    for part in x_parts:
        n_tiles = part.shape[0] // FFN_TM
        (y,) = _ffn(hn, w_ffn_in, w_ffn_out, x, norm_final, DEPTH - 1, 0, False, FFN_TM,
                    tile0, n_tiles)
        outs.append(y)
        tile0 += n_tiles
    return outs


def kernel(x_prompt, x_sample, norm_mix, norm_ffn, w_in, gate_bias, diff_lambda, diff_subln,
           w_proj_a, w_proj_b, w_out, w_ffn_in, w_ffn_out, norm_final):
    bp, seq, d = x_prompt.shape
    bs = x_sample.shape[0]
    assert x_sample.shape[1:] == (seq, d) and d == D_MODEL
    assert seq % (B_PATTERNS[-1][1] * DIL_TQ) == 0 and seq >= DIL_WIN
    y_prompt, y_sample = _trunk(
        [x_prompt.reshape(bp * seq, d), x_sample.reshape(bs * seq, d)], seq,
        norm_mix.reshape(DEPTH, 1, D_MODEL), norm_ffn.reshape(DEPTH, 1, D_MODEL),
        _prepare_w_in(w_in), gate_bias.reshape(DEPTH, 1, GATE_WIDTH),
        diff_lambda, diff_subln.reshape(DEPTH, 2 * A_HEAD_DIM, 1),
        w_proj_a.astype(BF16), w_proj_b.astype(BF16), w_out.astype(BF16),
        w_ffn_in.astype(BF16), w_ffn_out.astype(BF16), norm_final.reshape(1, D_MODEL))
    return (y_prompt.reshape(bp, seq, d), y_sample.reshape(bs, seq, d))
```

```python
import functools
import math

import jax
import jax.numpy as jnp
from jax import lax
from jax.experimental import pallas as pl
from jax.experimental.pallas import tpu as pltpu

D_MODEL = 2048
DEPTH = 4
A_HEADS = 8
A_HEAD_DIM = 64
A_WIDTH = A_HEADS * 2 * A_HEAD_DIM
B_PATTERNS = ((128, 1), (512, 4), (2048, 16))
B_GROUPS = len(B_PATTERNS)
B_HEADS = 4
B_HEAD_DIM = 128
B_WIDTH = B_GROUPS * B_HEADS * B_HEAD_DIM
B_OUT = B_HEADS * B_HEAD_DIM
QKV_WIDTH = 3 * A_WIDTH + 3 * B_WIDTH
GATE_WIDTH = 2 * D_MODEL
D_FF = 5632
ROPE_THETA = 500000.0
ROPE_FRAC = 4
NORM_EPS = 1e-6
MASK_VALUE = -1e30
LOG2_E = 1.4426950408889634

LANES = 128
BF16_ROWS = 16
MXU_WIDTH = 256
VMEM_LIMIT = 60 * 1024 * 1024

PROJ_TN = 512
GATE_TILES = 2
DIFF_TQ = 512
DIFF_HEADS = 2
DIL_UNROLL = 8
DIL_TQ = 128
DIL_HALF = 64
DIL_WIN = 256
MAX_ROW_STRIDE = 4
MIX_TM = 512
FFN_TM = 512
FFN_TM_WIDE = 768
FFN_TM_LAST = 1024
FFN_TF = 512
FFN_CHUNKS = 2
NORM_TM = 512

BF16 = jnp.bfloat16
F32 = jnp.float32


def _params(*sem):
    return pltpu.CompilerParams(dimension_semantics=sem, vmem_limit_bytes=VMEM_LIMIT)


def _sigmoid(x):
    return 0.5 * jnp.tanh(0.5 * x) + 0.5


def _rms(x, g):
    ms = jnp.mean(x * x, axis=-1, keepdims=True)
    return x * lax.rsqrt(ms + NORM_EPS) * g


def _stream_specs(parts, tm):
    specs, firsts, start = [], [], 0
    for arr in parts:
        n = arr.shape[0] // tm
        specs.append(pl.BlockSpec((tm, D_MODEL),
                                  lambda i, *_, start=start, n=n: (jnp.clip(i - start, 0, n - 1), 0)))
        firsts.append(start)
        start += n
    return specs, tuple(firsts)


def _stream_tile(refs, firsts, i):
    x = refs[0][...]
    for ref, first in zip(refs[1:], firsts[1:]):
        x = jnp.where(i >= first, ref[...], x)
    return x


def _rmsnorm_kernel(*refs, firsts):
    x_refs, (g_ref, x_ref, o_ref) = refs[:-3], refs[-3:]
    x = _stream_tile(x_refs, firsts, pl.program_id(0))
    x_ref[...] = x
    o_ref[...] = _rms(x, g_ref[...]).astype(o_ref.dtype)


def _rmsnorm(parts, g_all, layer):
    t = sum(p.shape[0] for p in parts)
    specs, firsts = _stream_specs(parts, NORM_TM)
    row = pl.BlockSpec((NORM_TM, D_MODEL), lambda i: (i, 0))
    return pl.pallas_call(
        functools.partial(_rmsnorm_kernel, firsts=firsts),
        grid=(t // NORM_TM,),
        in_specs=specs + [pl.BlockSpec((None, 1, D_MODEL), lambda i: (layer, 0, 0))],
        out_specs=[row, row],
        out_shape=[jax.ShapeDtypeStruct((t, D_MODEL), F32), jax.ShapeDtypeStruct((t, D_MODEL), BF16)],
        compiler_params=_params("parallel"),
        name="rmsnorm_in",
    )(*parts, g_all)


def _stride_stages(d):
    if d <= MAX_ROW_STRIDE:
        return d, 1
    assert d % MAX_ROW_STRIDE == 0 and d // MAX_ROW_STRIDE <= MAX_ROW_STRIDE
    return MAX_ROW_STRIDE, d // MAX_ROW_STRIDE


def _block_class(blk, d):
    s1, s2 = _stride_stages(d)
    return blk if s2 == 1 else s1 * (blk % s2) + blk // s2


def _proj_kernel(x_ref, w_ref, *rest, seq, rope, d, tn, scaled_tiles):
    if rope:
        c_ref, s_ref, o_ref, *scr = rest
    else:
        o_ref, *scr = rest
    j = pl.program_id(1)
    n_sub = seq // d
    x = x_ref[...]
    assert MXU_WIDTH == 2 * LANES
    for m in range(tn // MXU_WIDTH):
        acc = jnp.dot(x, w_ref[:, m * MXU_WIDTH:(m + 1) * MXU_WIDTH], preferred_element_type=F32)
        halves = [acc[:, :LANES], acc[:, LANES:]]
        if rope:
            halves = [h * c_ref[...] + pltpu.roll(h, LANES // 2, 1) * s_ref[...] for h in halves]
        for c, val in enumerate(halves):
            chunk = m * (MXU_WIDTH // LANES) + c
            sl = slice(chunk * LANES, (chunk + 1) * LANES)
            if scaled_tiles:
                val = val * jnp.where(j < scaled_tiles, A_HEAD_DIM ** -0.5, 1.0).astype(F32)
            if d == 1:
                o_ref[:, sl] = val.astype(o_ref.dtype)
                continue
            buf = scr[chunk]
            buf[...] = val
            s1, s2 = _stride_stages(d)
            if s2 > 1:
                tmp, n1 = scr[tn // LANES + chunk], seq // s1
                for p1 in range(s1):
                    tmp[p1 * n1:(p1 + 1) * n1, :] = buf[pl.ds(p1, n1, stride=s1), :]
                runs = [(tmp, p1 * n1, s2) for p1 in range(s1)]
            else:
                runs = [(buf, 0, s1)]
            blk = 0
            for src, base, stride in runs:
                for p in range(stride):
                    o_ref[blk * n_sub:(blk + 1) * n_sub, sl] = (
                        src[pl.ds(base + p, n_sub, stride=stride), :].astype(o_ref.dtype))
                    blk += 1


def _proj_class(xn, w_in, layer, seq, name, tiles, tn, rope=None, tables=None, d=1, scaled_tiles=0):
    t = xn.shape[0]

    def w_map(i, j):
        tile = tiles[0]
        for idx in range(1, len(tiles)):
            tile = jnp.where(j >= idx, tiles[idx], tile)
        return (layer, 0, tile)

    in_specs = [pl.BlockSpec((seq, D_MODEL), lambda i, j: (i, 0)),
                pl.BlockSpec((None, D_MODEL, tn), w_map)]
    args = [xn, w_in]
    if rope:
        in_specs += [pl.BlockSpec((seq, LANES), lambda i, j: (0, 0))] * 2
        args += list(tables)
    return pl.pallas_call(
        functools.partial(_proj_kernel, seq=seq, rope=rope, d=d, tn=tn, scaled_tiles=scaled_tiles),
        grid=(t // seq, len(tiles)),
        in_specs=in_specs,
        out_specs=pl.BlockSpec((seq, tn), lambda i, j: (i, j)),
        out_shape=jax.ShapeDtypeStruct((t, len(tiles) * tn), BF16),
        scratch_shapes=([pltpu.VMEM((seq, LANES), F32)]
                        * (tn // LANES * (0 if d == 1 else 1 if _stride_stages(d)[1] == 1 else 2))),
        compiler_params=_params("parallel", "arbitrary"),
        name=name,
    )(*args)


def _qkv_proj(xn, w_in, layer, tables, seq):
    tab_a, tab_b = tables[:2], tables[2:]
    wide = 2 * PROJ_TN
    n_a = 3 * A_WIDTH // PROJ_TN
    n_b = B_WIDTH // PROJ_TN
    assert n_b == B_GROUPS and A_WIDTH == wide
    proj = functools.partial(_proj_class, xn, w_in, layer, seq)
    a_qk = proj("proj_a_qk", (0, 1), wide, rope=True, tables=tab_a, scaled_tiles=1)
    v_plain = proj("proj_v_plain", (n_a - 2, n_a - 1, n_a + 2 * n_b), PROJ_TN)
    groups = []
    for gi, (_, d) in enumerate(B_PATTERNS):
        qk = proj(f"proj_b_qk_d{d}", (n_a + gi, n_a + n_b + gi), PROJ_TN,
                  rope=True, tables=tab_b, d=d)
        v = None if d == 1 else proj(f"proj_b_v_d{d}", (n_a + 2 * n_b + gi,), PROJ_TN, d=d)
        groups.append((qk, v))
    return a_qk, v_plain, groups


def _split_order(head_dim):
    half = head_dim // ROPE_FRAC // 2
    x1, x2, rest = [], [], []
    for base in range(0, LANES, head_dim):
        x1 += range(base, base + half)
        x2 += range(base + half, base + 2 * half)
        rest += range(base + 2 * half, base + head_dim)
    keep = LANES // 2 - len(x1)
    return x1 + rest[:keep] + x2 + rest[keep:]


def _lanes_of_first_head(lane, head_dim):
    order = _split_order(head_dim)
    mask, start = None, None
    for pos in range(LANES + 1):
        inside = pos < LANES and order[pos] < head_dim
        if inside and start is None:
            start = pos
        elif not inside and start is not None:
            run = (lane >= start) & (lane < pos)
            mask, start = run if mask is None else mask | run, None
    return mask


def _split_rope_tables(seq, head_dim):
    rot = head_dim // ROPE_FRAC
    heads = LANES // head_dim
    inv = ROPE_THETA ** (-(jnp.arange(0, rot, 2, dtype=F32) / rot))
    ang = jnp.arange(seq, dtype=F32)[:, None] * inv[None, :]
    cos, sin = jnp.tile(jnp.cos(ang), (1, heads)), jnp.tile(jnp.sin(ang), (1, heads))
    rest = LANES // 2 - cos.shape[1]
    one, zero = jnp.ones((seq, rest), F32), jnp.zeros((seq, rest), F32)
    return (jnp.concatenate([cos, one, cos, one], axis=-1),
            jnp.concatenate([-sin, zero, sin, zero], axis=-1))


def _w_in_kernel(w_ref, pa_ref, pb_ref, o_ref):
    j = pl.program_id(1)
    n_qa = 2 * A_WIDTH // PROJ_TN
    n_a = 3 * A_WIDTH // PROJ_TN
    n_b = B_WIDTH // PROJ_TN
    is_a = j < n_qa
    is_b = (j >= n_a) & (j < n_a + 2 * n_b)
    w = w_ref[...].astype(BF16)

    def reorder(p_ref):
        for m in range(PROJ_TN // MXU_WIDTH):
            cols = slice(m * MXU_WIDTH, (m + 1) * MXU_WIDTH)
            o_ref[:, cols] = jnp.dot(w[:, cols], p_ref[...],
                                     preferred_element_type=F32).astype(o_ref.dtype)

    pl.when(is_a)(lambda: reorder(pa_ref))
    pl.when(is_b)(lambda: reorder(pb_ref))

    @pl.when(jnp.logical_not(is_a | is_b))
    def _():
        o_ref[...] = w


def _reorder_matrix(head_dim):
    order = _split_order(head_dim)
    m = [[0.0] * MXU_WIDTH for _ in range(MXU_WIDTH)]
    for base in range(0, MXU_WIDTH, LANES):
        for new, old in enumerate(order):
            m[base + old][base + new] = 1.0
    return jnp.asarray(m, BF16)


def _prepare_w_in(w_in):
    depth, k, n = w_in.shape
    p_spec = pl.BlockSpec((MXU_WIDTH, MXU_WIDTH), lambda l, j: (0, 0))
    return pl.pallas_call(
        _w_in_kernel,
        grid=(depth, n // PROJ_TN),
        in_specs=[pl.BlockSpec((None, k, PROJ_TN), lambda l, j: (l, 0, j)), p_spec, p_spec],
        out_specs=pl.BlockSpec((None, k, PROJ_TN), lambda l, j: (l, 0, j)),
        out_shape=jax.ShapeDtypeStruct(w_in.shape, BF16),
        compiler_params=_params("parallel", "parallel"),
        name="w_in_prepare",
    )(w_in, _reorder_matrix(A_HEAD_DIM), _reorder_matrix(B_HEAD_DIM))


def _gate_kernel(x_ref, *refs):
    w_refs, (b_ref, o_ref) = refs[:-2], refs[-2:]
    x = x_ref[...]
    for n, w_ref in enumerate(w_refs):
        for m in range(PROJ_TN // MXU_WIDTH):
            cols = slice(m * MXU_WIDTH, (m + 1) * MXU_WIDTH)
            out_cols = slice(n * PROJ_TN + cols.start, n * PROJ_TN + cols.stop)
            z = jnp.dot(x, w_ref[:, cols], preferred_element_type=F32) + b_ref[:, out_cols]
            o_ref[:, out_cols] = _sigmoid(z)


def _gate_proj(xn, w_in, bias, layer, seq):
    t = xn.shape[0]
    col0 = QKV_WIDTH // PROJ_TN
    wide = GATE_TILES * PROJ_TN
    w_spec = lambda n: pl.BlockSpec((None, D_MODEL, PROJ_TN),
                                    lambda i, j: (layer, 0, col0 + GATE_TILES * j + n))
    return pl.pallas_call(
        _gate_kernel,
        grid=(t // seq, GATE_WIDTH // wide),
        in_specs=[pl.BlockSpec((seq, D_MODEL), lambda i, j: (i, 0))]
        + [w_spec(n) for n in range(GATE_TILES)]
        + [pl.BlockSpec((None, 1, wide), lambda i, j: (layer, 0, j))],
        out_specs=pl.BlockSpec((seq, wide), lambda i, j: (i, j)),
        out_shape=jax.ShapeDtypeStruct((t, GATE_WIDTH), F32),
        compiler_params=_params("parallel", "arbitrary"),
        name="gate_proj",
    )(xn, *([w_in] * GATE_TILES), bias)


def _nt_dot(a, b):
    return lax.dot_general(a, b, (((1,), (1,)), ((), ())), preferred_element_type=F32)


def _diff_kernel(q_ref, k_ref, v_ref, lam_ref, g_ref, o_ref, *vt_refs, seq, lam_init):
    lf = lam_ref[...]
    lam = (jnp.exp(jnp.sum(lf[0:1] * lf[1:2], axis=-1, keepdims=True))
           - jnp.exp(jnp.sum(lf[2:3] * lf[3:4], axis=-1, keepdims=True)) + lam_init)
    hd = 2 * A_HEAD_DIM
    head_cols = [slice(a * LANES, (a + 1) * LANES) for a in range(len(vt_refs))]
    for vt_ref, cols in zip(vt_refs, head_cols):
        vt_ref[0:hd, :] = v_ref[:, cols].astype(F32).T.astype(BF16)
        vt_ref[hd:, :] = jnp.ones((vt_ref.shape[0] - hd, seq), BF16)
    g = g_ref[...] * (1.0 - lam_init)
    lane = lax.broadcasted_iota(jnp.int32, (DIFF_TQ, LANES), 1)
    first = _lanes_of_first_head(lane, A_HEAD_DIM)

    def scores(a, t):
        q = q_ref[t * DIFF_TQ:(t + 1) * DIFF_TQ, head_cols[a]]
        k = k_ref[:, head_cols[a]]
        zero = jnp.zeros_like(q)
        return (_nt_dot(k, jnp.where(first, q, zero)),
                _nt_dot(k, jnp.where(first, zero, q)))

    items = [(a, t) for a in range(len(vt_refs)) for t in range(seq // DIFF_TQ)]
    s_next = scores(*items[0])
    for i, (a, t) in enumerate(items):
        s1, s2 = s_next
        if i + 1 < len(items):
            s_next = scores(*items[i + 1])
        e1 = jnp.exp(s1 - jnp.max(s1, axis=0, keepdims=True)).astype(BF16)
        e2 = jnp.exp(s2 - jnp.max(s2, axis=0, keepdims=True)).astype(BF16)
        vt = vt_refs[a][...]
        u1 = jnp.dot(vt, e1, preferred_element_type=F32)
        u2 = jnp.dot(vt, e2, preferred_element_type=F32)
        o = u1[:hd] * (1.0 / u1[hd:hd + 1]) - u2[:hd] * (lam / u2[hd:hd + 1])
        ms = jnp.mean(o * o, axis=0, keepdims=True)
        o_ref[t * DIFF_TQ:(t + 1) * DIFF_TQ, head_cols[a]] = (
            (o * lax.rsqrt(ms + NORM_EPS) * g).T.astype(o_ref.dtype))


def _diff_attention(a_qk, v_plain, lam_all, subln_all, layer, seq):
    t = a_qk.shape[0]
    hd = 2 * A_HEAD_DIM
    width = DIFF_HEADS * LANES
    k_off = A_WIDTH // width
    blk = lambda off: pl.BlockSpec((seq, width), lambda b, h: (b, off + h))
    return pl.pallas_call(
        functools.partial(_diff_kernel, seq=seq, lam_init=0.8 - 0.6 * math.exp(-0.3 * layer)),
        grid=(t // seq, A_HEADS // DIFF_HEADS),
        in_specs=[blk(0), blk(k_off), blk(0),
                  pl.BlockSpec((None, 4, A_HEAD_DIM), lambda b, h: (layer, 0, 0)),
                  pl.BlockSpec((None, hd, 1), lambda b, h: (layer, 0, 0))],
        out_specs=blk(0),
        out_shape=jax.ShapeDtypeStruct((t, A_WIDTH), BF16),
        scratch_shapes=[pltpu.VMEM((hd + BF16_ROWS, seq), BF16)] * DIFF_HEADS,
        compiler_params=_params("parallel", "parallel"),
        name="diff_attention",
    )(a_qk, a_qk, v_plain, lam_all, subln_all)


def _dil_kernel(*refs, seq):
    q_refs, k_refs, v_refs = refs[0:3], refs[3:6], refs[6:9]
    o_ref = refs[9]
    o_scr, l_scr = refs[10:13], refs[13:16]
    scale = B_HEAD_DIM ** -0.5
    row = lax.broadcasted_iota(jnp.int32, (DIL_TQ, DIL_WIN), 0)
    col = lax.broadcasted_iota(jnp.int32, (DIL_TQ, DIL_WIN), 1)
    rel = col - row
    col1 = lax.broadcasted_iota(jnp.int32, (1, DIL_WIN), 1)

    def step(t, carry):
        q0 = pl.multiple_of(t * DIL_TQ, DIL_TQ)
        ws = pl.multiple_of(jnp.clip(q0 - DIL_HALF, 0, seq - DIL_WIN), DIL_HALF)
        band = jnp.abs(rel + (ws - q0)) <= DIL_HALF
        for gi, (window, d) in enumerate(B_PATTERNS):
            assert window // (2 * d) == DIL_HALF
            n_sub = seq // d
            nblk = n_sub // DIL_TQ
            run = t // nblk
            n = t % nblk
            lo = run * n_sub - ws
            p = _block_class(run, d)
            q = q_refs[gi][pl.ds(q0, DIL_TQ), :]
            k = k_refs[gi][pl.ds(ws, DIL_WIN), :]
            v = v_refs[gi][pl.ds(ws, DIL_WIN), :]
            s = jnp.where(band, _nt_dot(q, k), MASK_VALUE)
            if d > 1:
                s = s + jnp.where((col1 >= lo) & (col1 < lo + n_sub), 0.0, MASK_VALUE)
            m = jnp.max(s, axis=-1, keepdims=True)
            e = jnp.exp2((s - m) * (scale * LOG2_E))
            l = jnp.sum(e, axis=-1, keepdims=True)
            o = jnp.dot(e.astype(v.dtype), v, preferred_element_type=F32) / l
            lse = jnp.broadcast_to(m * scale + jnp.log(l), (DIL_TQ, LANES))
            dst = pl.ds(n * (DIL_TQ * d) + p, DIL_TQ, stride=d) if d > 1 else pl.ds(q0, DIL_TQ)
            o_scr[gi][dst, :] = o
            l_scr[gi][dst, :] = lse
        return carry

    lax.fori_loop(0, seq // DIL_TQ, step, 0, unroll=DIL_UNROLL)

    l0, l1, l2 = l_scr[0][...], l_scr[1][...], l_scr[2][...]
    m = jnp.maximum(jnp.maximum(l0, l1), l2)
    e0, e1, e2 = jnp.exp(l0 - m), jnp.exp(l1 - m), jnp.exp(l2 - m)
    num = e0 * o_scr[0][...] + e1 * o_scr[1][...] + e2 * o_scr[2][...]
    o_ref[...] = (num / (e0 + e1 + e2)).astype(o_ref.dtype)


def _dil_attention(v_plain, groups, seq):
    t = v_plain.shape[0]
    blk = lambda off: pl.BlockSpec((seq, LANES), lambda b, h: (b, off + h))
    qs = [qk for qk, _ in groups]
    vs = [v_plain if v is None else v for _, v in groups]
    v_off = [A_WIDTH // LANES if v is None else 0 for _, v in groups]
    in_specs = ([blk(0)] * B_GROUPS + [blk(B_HEADS)] * B_GROUPS + [blk(off) for off in v_off])
    return pl.pallas_call(
        functools.partial(_dil_kernel, seq=seq),
        grid=(t // seq, B_HEADS),
        in_specs=in_specs,
        out_specs=pl.BlockSpec((seq, LANES), lambda b, h: (b, h)),
        out_shape=jax.ShapeDtypeStruct((t, B_OUT), BF16),
        scratch_shapes=[pltpu.VMEM((seq, LANES), F32)] * (2 * B_GROUPS),
        compiler_params=_params("parallel", "parallel"),
        name="dilated_attention",
    )(*qs, *qs, *vs)


def _mix_kernel(oa_ref, ob_ref, ga_ref, gb_ref, x_ref, wpa_ref, wpb_ref, wo_ref, g_ref,
                xo_ref, hn_ref):
    ya = jnp.dot(oa_ref[...], wpa_ref[...], preferred_element_type=F32)
    yb = jnp.dot(ob_ref[...], wpb_ref[...], preferred_element_type=F32)
    merged = (ga_ref[...] * ya + gb_ref[...] * yb).astype(BF16)
    xo = x_ref[...] + jnp.dot(merged, wo_ref[...], preferred_element_type=F32)
    xo_ref[...] = xo
    hn_ref[...] = _rms(xo, g_ref[...]).astype(hn_ref.dtype)


def _mix_out(oa, ob, gates, x, w_pa, w_pb, w_out, norm_ffn, layer):
    t = x.shape[0]
    row = lambda width, col=0: pl.BlockSpec((MIX_TM, width), lambda i: (i, col))
    whole = lambda k, n: pl.BlockSpec((None, k, n), lambda i: (layer, 0, 0),
                                      pipeline_mode=pl.Buffered(1))
    return pl.pallas_call(
        _mix_kernel,
        grid=(t // MIX_TM,),
        in_specs=[row(A_WIDTH), row(B_OUT), row(D_MODEL, 0), row(D_MODEL, 1), row(D_MODEL),
                  whole(A_WIDTH, D_MODEL), whole(B_OUT, D_MODEL), whole(D_MODEL, D_MODEL),
                  pl.BlockSpec((None, 1, D_MODEL), lambda i: (layer, 0, 0))],
        out_specs=[row(D_MODEL), row(D_MODEL)],
        out_shape=[jax.ShapeDtypeStruct((t, D_MODEL), F32),
                   jax.ShapeDtypeStruct((t, D_MODEL), BF16)],
        compiler_params=_params("parallel"),
        name="mix_out",
    )(oa, ob, gates, gates, x, w_pa, w_pb, w_out, norm_ffn)


def _ffn_kernel(hn_ref, wg_ref, wu_ref, w2_ref, x_ref, g_ref, *outs):
    acc_ref = outs[0]
    k = pl.program_id(1)

    @pl.when(k == 0)
    def _():
        acc_ref[...] = x_ref[...]

    hn = hn_ref[...]
    width = FFN_TF // FFN_CHUNKS
    acts = []
    for c in range(FFN_CHUNKS):
        cols = slice(c * width, (c + 1) * width)
        hg = jnp.dot(hn, wg_ref[:, cols], preferred_element_type=F32)
        hu = jnp.dot(hn, wu_ref[:, cols], preferred_element_type=F32)
        acts.append((hg * _sigmoid(hg) * hu).astype(BF16))
    part = None
    for c in range(FFN_CHUNKS):
        rows = slice(c * width, (c + 1) * width)
        p = jnp.dot(acts[c], w2_ref[rows, :], preferred_element_type=F32)
        part = p if part is None else part + p
    acc_ref[...] += part

    @pl.when(k == pl.num_programs(1) - 1)
    def _():
        outs[-1][...] = _rms(acc_ref[...], g_ref[...]).astype(outs[-1].dtype)


def _ffn(hn, w_ffn_in, w_ffn_out, x, norm_all, layer, norm_layer, emit_x, tm, tile0=0, n_tiles=None):
    if n_tiles is None:
        n_tiles = x.shape[0] // tm
    t = n_tiles * tm
    nk = D_FF // FFN_TF
    row_in = pl.BlockSpec((tm, D_MODEL), lambda i, k: (tile0 + i, 0))
    row_out = pl.BlockSpec((tm, D_MODEL), lambda i, k: (i, 0))
    if norm_all.ndim == 3:
        g_spec = pl.BlockSpec((None, 1, D_MODEL), lambda i, k: (norm_layer, 0, 0))
    else:
        g_spec = pl.BlockSpec((1, D_MODEL), lambda i, k: (0, 0))
    out_specs = [row_out]
    out_shape = [jax.ShapeDtypeStruct((t, D_MODEL), F32)]
    if emit_x:
        out_specs = [row_out, row_out]
        out_shape = out_shape + [jax.ShapeDtypeStruct((t, D_MODEL), BF16)]
    return pl.pallas_call(
        _ffn_kernel,
        grid=(n_tiles, nk),
        in_specs=[row_in,
                  pl.BlockSpec((None, D_MODEL, FFN_TF), lambda i, k: (layer, 0, k)),
                  pl.BlockSpec((None, D_MODEL, FFN_TF), lambda i, k: (layer, 0, nk + k)),
                  pl.BlockSpec((None, FFN_TF, D_MODEL), lambda i, k: (layer, k, 0)),
                  row_in, g_spec],
        out_specs=out_specs,
        out_shape=out_shape,
        compiler_params=_params("parallel", "arbitrary"),
        name="ffn",
    )(hn, w_ffn_in, w_ffn_in, w_ffn_out, x, norm_all)


def _trunk(x_parts, seq, norm_mix, norm_ffn, w_in, gate_bias, diff_lambda, diff_subln,
           w_pa, w_pb, w_out, w_ffn_in, w_ffn_out, norm_final):
    tables = _split_rope_tables(seq, A_HEAD_DIM) + _split_rope_tables(seq, B_HEAD_DIM)
    x, xn = _rmsnorm(x_parts, norm_mix, 0)
    for layer in range(DEPTH):
        a_qk, v_plain, groups = _qkv_proj(xn, w_in, layer, tables, seq)
        gates = _gate_proj(xn, w_in, gate_bias, layer, seq)
        oa = _diff_attention(a_qk, v_plain, diff_lambda, diff_subln, layer, seq)
        ob = _dil_attention(v_plain, groups, seq)
        x, hn = _mix_out(oa, ob, gates, x, w_pa, w_pb, w_out, norm_ffn, layer)
        if layer + 1 < DEPTH:
            tm = FFN_TM_WIDE if x.shape[0] % FFN_TM_WIDE == 0 else FFN_TM
            x, xn = _ffn(hn, w_ffn_in, w_ffn_out, x, norm_mix, layer, layer + 1, True, tm)
    tm = FFN_TM_LAST if all(p.shape[0] % FFN_TM_LAST == 0 for p in x_parts) else FFN_TM
    outs, tile0 = [], 0
    for part in x_parts:
        n_tiles = part.shape[0] // tm
        (y,) = _ffn(hn, w_ffn_in, w_ffn_out, x, norm_final, DEPTH - 1, 0, False, tm, tile0, n_tiles)
        outs.append(y)
        tile0 += n_tiles
    return outs


def kernel(x_prompt, x_sample, norm_mix, norm_ffn, w_in, gate_bias, diff_lambda, diff_subln,
           w_proj_a, w_proj_b, w_out, w_ffn_in, w_ffn_out, norm_final):
    bp, seq, d = x_prompt.shape
    bs = x_sample.shape[0]
    assert x_sample.shape[1:] == (seq, d) and d == D_MODEL
    assert seq % (B_PATTERNS[-1][1] * DIL_TQ) == 0 and seq >= DIL_WIN
    y_prompt, y_sample = _trunk(
        [x_prompt.reshape(bp * seq, d), x_sample.reshape(bs * seq, d)], seq,
        norm_mix.reshape(DEPTH, 1, D_MODEL), norm_ffn.reshape(DEPTH, 1, D_MODEL),
        _prepare_w_in(w_in), gate_bias.reshape(DEPTH, 1, GATE_WIDTH),
        diff_lambda, diff_subln.reshape(DEPTH, 2 * A_HEAD_DIM, 1),
        w_proj_a.astype(BF16), w_proj_b.astype(BF16), w_out.astype(BF16),
        w_ffn_in.astype(BF16), w_ffn_out.astype(BF16), norm_final.reshape(1, D_MODEL))
    return (y_prompt.reshape(bp, seq, d), y_sample.reshape(bs, seq, d))
```

```python
import functools
import math

import jax
import jax.numpy as jnp
from jax import lax
from jax.experimental import pallas as pl
from jax.experimental.pallas import tpu as pltpu

D_MODEL = 2048
DEPTH = 4
A_HEADS = 8
A_HEAD_DIM = 64
A_WIDTH = A_HEADS * 2 * A_HEAD_DIM
B_PATTERNS = ((128, 1), (512, 4), (2048, 16))
B_GROUPS = len(B_PATTERNS)
B_HEADS = 4
B_HEAD_DIM = 128
B_WIDTH = B_GROUPS * B_HEADS * B_HEAD_DIM
B_OUT = B_HEADS * B_HEAD_DIM
QKV_WIDTH = 3 * A_WIDTH + 3 * B_WIDTH
GATE_WIDTH = 2 * D_MODEL
D_FF = 5632
ROPE_THETA = 500000.0
ROPE_FRAC = 4
NORM_EPS = 1e-6
MASK_VALUE = -1e30
LOG2_E = 1.4426950408889634

LANES = 128
BF16_ROWS = 16
MXU_WIDTH = 256
VMEM_LIMIT = 60 * 1024 * 1024

PROJ_TN = 512
GATE_TILES = 2
DIFF_TQ = 512
DIFF_HEADS = 2
DIL_UNROLL = 16
DIL_TQ = 128
DIL_HALF = 64
DIL_WIN = 256
MAX_ROW_STRIDE = 4
MIX_TM = 512
FFN_TM = 512
FFN_TM_WIDE = 768
FFN_TM_LAST = 1024
FFN_TF = 512
FFN_CHUNKS = 2
NORM_TM = 512

BF16 = jnp.bfloat16
F32 = jnp.float32


def _params(*sem):
    return pltpu.CompilerParams(dimension_semantics=sem, vmem_limit_bytes=VMEM_LIMIT)


def _sigmoid(x):
    return 0.5 * jnp.tanh(0.5 * x) + 0.5


def _rms(x, g):
    ms = jnp.mean(x * x, axis=-1, keepdims=True)
    return x * lax.rsqrt(ms + NORM_EPS) * g


def _stream_specs(parts, tm):
    specs, firsts, start = [], [], 0
    for arr in parts:
        n = arr.shape[0] // tm
        specs.append(pl.BlockSpec((tm, D_MODEL),
                                  lambda i, *_, start=start, n=n: (jnp.clip(i - start, 0, n - 1), 0)))
        firsts.append(start)
        start += n
    return specs, tuple(firsts)


def _stream_tile(refs, firsts, i):
    x = refs[0][...]
    for ref, first in zip(refs[1:], firsts[1:]):
        x = jnp.where(i >= first, ref[...], x)
    return x


def _rmsnorm_kernel(*refs, firsts):
    x_refs, (g_ref, x_ref, o_ref) = refs[:-3], refs[-3:]
    x = _stream_tile(x_refs, firsts, pl.program_id(0))
    x_ref[...] = x
    o_ref[...] = _rms(x, g_ref[...]).astype(o_ref.dtype)


def _rmsnorm(parts, g_all, layer):
    t = sum(p.shape[0] for p in parts)
    specs, firsts = _stream_specs(parts, NORM_TM)
    row = pl.BlockSpec((NORM_TM, D_MODEL), lambda i: (i, 0))
    return pl.pallas_call(
        functools.partial(_rmsnorm_kernel, firsts=firsts),
        grid=(t // NORM_TM,),
        in_specs=specs + [pl.BlockSpec((None, 1, D_MODEL), lambda i: (layer, 0, 0))],
        out_specs=[row, row],
        out_shape=[jax.ShapeDtypeStruct((t, D_MODEL), F32), jax.ShapeDtypeStruct((t, D_MODEL), BF16)],
        compiler_params=_params("parallel"),
        name="rmsnorm_in",
    )(*parts, g_all)


def _stride_stages(d):
    if d <= MAX_ROW_STRIDE:
        return d, 1
    assert d % MAX_ROW_STRIDE == 0 and d // MAX_ROW_STRIDE <= MAX_ROW_STRIDE
    return MAX_ROW_STRIDE, d // MAX_ROW_STRIDE


def _block_class(blk, d):
    s1, s2 = _stride_stages(d)
    return blk if s2 == 1 else s1 * (blk % s2) + blk // s2


def _proj_kernel(x_ref, w_ref, *rest, seq, rope, d, tn, scaled_tiles):
    if rope:
        c_ref, s_ref, o_ref, *scr = rest
    else:
        o_ref, *scr = rest
    j = pl.program_id(1)
    n_sub = seq // d
    x = x_ref[...]
    assert MXU_WIDTH == 2 * LANES
    for m in range(tn // MXU_WIDTH):
        acc = jnp.dot(x, w_ref[:, m * MXU_WIDTH:(m + 1) * MXU_WIDTH], preferred_element_type=F32)
        halves = [acc[:, :LANES], acc[:, LANES:]]
        if rope:
            halves = [h * c_ref[...] + pltpu.roll(h, LANES // 2, 1) * s_ref[...] for h in halves]
        for c, val in enumerate(halves):
            chunk = m * (MXU_WIDTH // LANES) + c
            sl = slice(chunk * LANES, (chunk + 1) * LANES)
            if scaled_tiles:
                val = val * jnp.where(j < scaled_tiles, A_HEAD_DIM ** -0.5, 1.0).astype(F32)
            if d == 1:
                o_ref[:, sl] = val.astype(o_ref.dtype)
                continue
            buf = scr[chunk]
            buf[...] = val
            s1, s2 = _stride_stages(d)
            if s2 > 1:
                tmp, n1 = scr[tn // LANES + chunk], seq // s1
                for p1 in range(s1):
                    tmp[p1 * n1:(p1 + 1) * n1, :] = buf[pl.ds(p1, n1, stride=s1), :]
                runs = [(tmp, p1 * n1, s2) for p1 in range(s1)]
            else:
                runs = [(buf, 0, s1)]
            blk = 0
            for src, base, stride in runs:
                for p in range(stride):
                    o_ref[blk * n_sub:(blk + 1) * n_sub, sl] = (
                        src[pl.ds(base + p, n_sub, stride=stride), :].astype(o_ref.dtype))
                    blk += 1


def _proj_class(xn, w_in, layer, seq, name, tiles, tn, rope=None, tables=None, d=1, scaled_tiles=0):
    t = xn.shape[0]

    def w_map(i, j):
        tile = tiles[0]
        for idx in range(1, len(tiles)):
            tile = jnp.where(j >= idx, tiles[idx], tile)
        return (layer, 0, tile)

    in_specs = [pl.BlockSpec((seq, D_MODEL), lambda i, j: (i, 0)),
                pl.BlockSpec((None, D_MODEL, tn), w_map)]
    args = [xn, w_in]
    if rope:
        in_specs += [pl.BlockSpec((seq, LANES), lambda i, j: (0, 0))] * 2
        args += list(tables)
    return pl.pallas_call(
        functools.partial(_proj_kernel, seq=seq, rope=rope, d=d, tn=tn, scaled_tiles=scaled_tiles),
        grid=(t // seq, len(tiles)),
        in_specs=in_specs,
        out_specs=pl.BlockSpec((seq, tn), lambda i, j: (i, j)),
        out_shape=jax.ShapeDtypeStruct((t, len(tiles) * tn), BF16),
        scratch_shapes=([pltpu.VMEM((seq, LANES), F32)]
                        * (tn // LANES * (0 if d == 1 else 1 if _stride_stages(d)[1] == 1 else 2))),
        compiler_params=_params("parallel", "arbitrary"),
        name=name,
    )(*args)


def _qkv_proj(xn, w_in, layer, tables, seq):
    tab_a, tab_b = tables[:2], tables[2:]
    wide = 2 * PROJ_TN
    n_a = 3 * A_WIDTH // PROJ_TN
    n_b = B_WIDTH // PROJ_TN
    assert n_b == B_GROUPS and A_WIDTH == wide
    proj = functools.partial(_proj_class, xn, w_in, layer, seq)
    a_qk = proj("proj_a_qk", (0, 1), wide, rope=True, tables=tab_a, scaled_tiles=1)
    v_plain = proj("proj_v_plain", (n_a - 2, n_a - 1, n_a + 2 * n_b), PROJ_TN)
    groups = []
    for gi, (_, d) in enumerate(B_PATTERNS):
        qk = proj(f"proj_b_qk_d{d}", (n_a + gi, n_a + n_b + gi), PROJ_TN,
                  rope=True, tables=tab_b, d=d)
        v = None if d == 1 else proj(f"proj_b_v_d{d}", (n_a + 2 * n_b + gi,), PROJ_TN, d=d)
        groups.append((qk, v))
    return a_qk, v_plain, groups


def _split_order(head_dim):
    half = head_dim // ROPE_FRAC // 2
    x1, x2, rest = [], [], []
    for base in range(0, LANES, head_dim):
        x1 += range(base, base + half)
        x2 += range(base + half, base + 2 * half)
        rest += range(base + 2 * half, base + head_dim)
    keep = LANES // 2 - len(x1)
    return x1 + rest[:keep] + x2 + rest[keep:]


def _lanes_of_first_head(lane, head_dim):
    order = _split_order(head_dim)
    mask, start = None, None
    for pos in range(LANES + 1):
        inside = pos < LANES and order[pos] < head_dim
        if inside and start is None:
            start = pos
        elif not inside and start is not None:
            run = (lane >= start) & (lane < pos)
            mask, start = run if mask is None else mask | run, None
    return mask


def _split_rope_tables(seq, head_dim):
    rot = head_dim // ROPE_FRAC
    heads = LANES // head_dim
    inv = ROPE_THETA ** (-(jnp.arange(0, rot, 2, dtype=F32) / rot))
    ang = jnp.arange(seq, dtype=F32)[:, None] * inv[None, :]
    cos, sin = jnp.tile(jnp.cos(ang), (1, heads)), jnp.tile(jnp.sin(ang), (1, heads))
    rest = LANES // 2 - cos.shape[1]
    one, zero = jnp.ones((seq, rest), F32), jnp.zeros((seq, rest), F32)
    return (jnp.concatenate([cos, one, cos, one], axis=-1),
            jnp.concatenate([-sin, zero, sin, zero], axis=-1))


def _w_in_kernel(w_ref, pa_ref, pb_ref, o_ref):
    j = pl.program_id(1)
    n_qa = 2 * A_WIDTH // PROJ_TN
    n_a = 3 * A_WIDTH // PROJ_TN
    n_b = B_WIDTH // PROJ_TN
    is_a = j < n_qa
    is_b = (j >= n_a) & (j < n_a + 2 * n_b)
    w = w_ref[...].astype(BF16)

    def reorder(p_ref):
        for m in range(PROJ_TN // MXU_WIDTH):
            cols = slice(m * MXU_WIDTH, (m + 1) * MXU_WIDTH)
            o_ref[:, cols] = jnp.dot(w[:, cols], p_ref[...],
                                     preferred_element_type=F32).astype(o_ref.dtype)

    pl.when(is_a)(lambda: reorder(pa_ref))
    pl.when(is_b)(lambda: reorder(pb_ref))

    @pl.when(jnp.logical_not(is_a | is_b))
    def _():
        o_ref[...] = w


def _reorder_matrix(head_dim):
    order = _split_order(head_dim)
    m = [[0.0] * MXU_WIDTH for _ in range(MXU_WIDTH)]
    for base in range(0, MXU_WIDTH, LANES):
        for new, old in enumerate(order):
            m[base + old][base + new] = 1.0
    return jnp.asarray(m, BF16)


def _prepare_w_in(w_in):
    depth, k, n = w_in.shape
    p_spec = pl.BlockSpec((MXU_WIDTH, MXU_WIDTH), lambda l, j: (0, 0))
    return pl.pallas_call(
        _w_in_kernel,
        grid=(depth, n // PROJ_TN),
        in_specs=[pl.BlockSpec((None, k, PROJ_TN), lambda l, j: (l, 0, j)), p_spec, p_spec],
        out_specs=pl.BlockSpec((None, k, PROJ_TN), lambda l, j: (l, 0, j)),
        out_shape=jax.ShapeDtypeStruct(w_in.shape, BF16),
        compiler_params=_params("parallel", "parallel"),
        name="w_in_prepare",
    )(w_in, _reorder_matrix(A_HEAD_DIM), _reorder_matrix(B_HEAD_DIM))


def _gate_kernel(x_ref, *refs):
    w_refs, (b_ref, o_ref) = refs[:-2], refs[-2:]
    x = x_ref[...]
    for n, w_ref in enumerate(w_refs):
        for m in range(PROJ_TN // MXU_WIDTH):
            cols = slice(m * MXU_WIDTH, (m + 1) * MXU_WIDTH)
            out_cols = slice(n * PROJ_TN + cols.start, n * PROJ_TN + cols.stop)
            z = jnp.dot(x, w_ref[:, cols], preferred_element_type=F32) + b_ref[:, out_cols]
            o_ref[:, out_cols] = _sigmoid(z)


def _gate_proj(xn, w_in, bias, layer, seq):
    t = xn.shape[0]
    col0 = QKV_WIDTH // PROJ_TN
    wide = GATE_TILES * PROJ_TN
    w_spec = lambda n: pl.BlockSpec((None, D_MODEL, PROJ_TN),
                                    lambda i, j: (layer, 0, col0 + GATE_TILES * j + n))
    return pl.pallas_call(
        _gate_kernel,
        grid=(t // seq, GATE_WIDTH // wide),
        in_specs=[pl.BlockSpec((seq, D_MODEL), lambda i, j: (i, 0))]
        + [w_spec(n) for n in range(GATE_TILES)]
        + [pl.BlockSpec((None, 1, wide), lambda i, j: (layer, 0, j))],
        out_specs=pl.BlockSpec((seq, wide), lambda i, j: (i, j)),
        out_shape=jax.ShapeDtypeStruct((t, GATE_WIDTH), F32),
        compiler_params=_params("parallel", "arbitrary"),
        name="gate_proj",
    )(xn, *([w_in] * GATE_TILES), bias)


def _nt_dot(a, b):
    return lax.dot_general(a, b, (((1,), (1,)), ((), ())), preferred_element_type=F32)


def _diff_kernel(q_ref, k_ref, v_ref, lam_ref, g_ref, o_ref, *vt_refs, seq, lam_init):
    lf = lam_ref[...]
    lam = (jnp.exp(jnp.sum(lf[0:1] * lf[1:2], axis=-1, keepdims=True))
           - jnp.exp(jnp.sum(lf[2:3] * lf[3:4], axis=-1, keepdims=True)) + lam_init)
    hd = 2 * A_HEAD_DIM
    head_cols = [slice(a * LANES, (a + 1) * LANES) for a in range(len(vt_refs))]
    for vt_ref, cols in zip(vt_refs, head_cols):
        vt_ref[0:hd, :] = v_ref[:, cols].astype(F32).T.astype(BF16)
        vt_ref[hd:, :] = jnp.ones((vt_ref.shape[0] - hd, seq), BF16)
    g = g_ref[...] * (1.0 - lam_init)
    lane = lax.broadcasted_iota(jnp.int32, (DIFF_TQ, LANES), 1)
    first = _lanes_of_first_head(lane, A_HEAD_DIM)

    def scores(a, t):
        q = q_ref[t * DIFF_TQ:(t + 1) * DIFF_TQ, head_cols[a]]
        k = k_ref[:, head_cols[a]]
        zero = jnp.zeros_like(q)
        return (_nt_dot(k, jnp.where(first, q, zero)),
                _nt_dot(k, jnp.where(first, zero, q)))

    items = [(a, t) for a in range(len(vt_refs)) for t in range(seq // DIFF_TQ)]
    s_next = scores(*items[0])
    for i, (a, t) in enumerate(items):
        s1, s2 = s_next
        if i + 1 < len(items):
            s_next = scores(*items[i + 1])
        e1 = jnp.exp(s1 - jnp.max(s1, axis=0, keepdims=True)).astype(BF16)
        e2 = jnp.exp(s2 - jnp.max(s2, axis=0, keepdims=True)).astype(BF16)
        vt = vt_refs[a][...]
        u1 = jnp.dot(vt, e1, preferred_element_type=F32)
        u2 = jnp.dot(vt, e2, preferred_element_type=F32)
        o = u1[:hd] * (1.0 / u1[hd:hd + 1]) - u2[:hd] * (lam / u2[hd:hd + 1])
        ms = jnp.mean(o * o, axis=0, keepdims=True)
        o_ref[t * DIFF_TQ:(t + 1) * DIFF_TQ, head_cols[a]] = (
            (o * lax.rsqrt(ms + NORM_EPS) * g).T.astype(o_ref.dtype))


def _diff_attention(a_qk, v_plain, lam_all, subln_all, layer, seq):
    t = a_qk.shape[0]
    hd = 2 * A_HEAD_DIM
    width = DIFF_HEADS * LANES
    k_off = A_WIDTH // width
    blk = lambda off: pl.BlockSpec((seq, width), lambda b, h: (b, off + h))
    return pl.pallas_call(
        functools.partial(_diff_kernel, seq=seq, lam_init=0.8 - 0.6 * math.exp(-0.3 * layer)),
        grid=(t // seq, A_HEADS // DIFF_HEADS),
        in_specs=[blk(0), blk(k_off), blk(0),
                  pl.BlockSpec((None, 4, A_HEAD_DIM), lambda b, h: (layer, 0, 0)),
                  pl.BlockSpec((None, hd, 1), lambda b, h: (layer, 0, 0))],
        out_specs=blk(0),
        out_shape=jax.ShapeDtypeStruct((t, A_WIDTH), BF16),
        scratch_shapes=[pltpu.VMEM((hd + BF16_ROWS, seq), BF16)] * DIFF_HEADS,
        compiler_params=_params("parallel", "parallel"),
        name="diff_attention",
    )(a_qk, a_qk, v_plain, lam_all, subln_all)


def _dil_kernel(*refs, seq):
    q_refs, k_refs, v_refs = refs[0:3], refs[3:6], refs[6:9]
    o_ref = refs[9]
    o_scr, l_scr = refs[10:13], refs[13:16]
    scale = B_HEAD_DIM ** -0.5
    row = lax.broadcasted_iota(jnp.int32, (DIL_TQ, DIL_WIN), 0)
    col = lax.broadcasted_iota(jnp.int32, (DIL_TQ, DIL_WIN), 1)
    rel = col - row
    col1 = lax.broadcasted_iota(jnp.int32, (1, DIL_WIN), 1)

    def step(t, carry):
        q0 = pl.multiple_of(t * DIL_TQ, DIL_TQ)
        ws = pl.multiple_of(jnp.clip(q0 - DIL_HALF, 0, seq - DIL_WIN), DIL_HALF)
        band = jnp.abs(rel + (ws - q0)) <= DIL_HALF
        for gi, (window, d) in enumerate(B_PATTERNS):
            assert window // (2 * d) == DIL_HALF
            n_sub = seq // d
            nblk = n_sub // DIL_TQ
            run = t // nblk
            n = t % nblk
            lo = run * n_sub - ws
            p = _block_class(run, d)
            q = q_refs[gi][pl.ds(q0, DIL_TQ), :]
            k = k_refs[gi][pl.ds(ws, DIL_WIN), :]
            v = v_refs[gi][pl.ds(ws, DIL_WIN), :]
            s = jnp.where(band, _nt_dot(q, k), MASK_VALUE)
            if d > 1:
                s = s + jnp.where((col1 >= lo) & (col1 < lo + n_sub), 0.0, MASK_VALUE)
            m = jnp.max(s, axis=-1, keepdims=True)
            e = jnp.exp2((s - m) * (scale * LOG2_E))
            l = jnp.sum(e, axis=-1, keepdims=True)
            o = jnp.dot(e.astype(v.dtype), v, preferred_element_type=F32) / l
            lse = jnp.broadcast_to(m * scale + jnp.log(l), (DIL_TQ, LANES))
            dst = pl.ds(n * (DIL_TQ * d) + p, DIL_TQ, stride=d) if d > 1 else pl.ds(q0, DIL_TQ)
            o_scr[gi][dst, :] = o
            l_scr[gi][dst, :] = lse
        return carry

    lax.fori_loop(0, seq // DIL_TQ, step, 0, unroll=DIL_UNROLL)

    l0, l1, l2 = l_scr[0][...], l_scr[1][...], l_scr[2][...]
    m = jnp.maximum(jnp.maximum(l0, l1), l2)
    e0, e1, e2 = jnp.exp(l0 - m), jnp.exp(l1 - m), jnp.exp(l2 - m)
    num = e0 * o_scr[0][...] + e1 * o_scr[1][...] + e2 * o_scr[2][...]
    o_ref[...] = (num / (e0 + e1 + e2)).astype(o_ref.dtype)


def _dil_attention(v_plain, groups, seq):
    t = v_plain.shape[0]
    blk = lambda off: pl.BlockSpec((seq, LANES), lambda b, h: (b, off + h))
    qs = [qk for qk, _ in groups]
    vs = [v_plain if v is None else v for _, v in groups]
    v_off = [A_WIDTH // LANES if v is None else 0 for _, v in groups]
    in_specs = ([blk(0)] * B_GROUPS + [blk(B_HEADS)] * B_GROUPS + [blk(off) for off in v_off])
    return pl.pallas_call(
        functools.partial(_dil_kernel, seq=seq),
        grid=(t // seq, B_HEADS),
        in_specs=in_specs,
        out_specs=pl.BlockSpec((seq, LANES), lambda b, h: (b, h)),
        out_shape=jax.ShapeDtypeStruct((t, B_OUT), BF16),
        scratch_shapes=[pltpu.VMEM((seq, LANES), F32)] * (2 * B_GROUPS),
        compiler_params=_params("parallel", "parallel"),
        name="dilated_attention",
    )(*qs, *qs, *vs)


def _mix_kernel(oa_ref, ob_ref, ga_ref, gb_ref, x_ref, wpa_ref, wpb_ref, wo_ref, g_ref,
                xo_ref, hn_ref):
    ya = jnp.dot(oa_ref[...], wpa_ref[...], preferred_element_type=F32)
    yb = jnp.dot(ob_ref[...], wpb_ref[...], preferred_element_type=F32)
    merged = (ga_ref[...] * ya + gb_ref[...] * yb).astype(BF16)
    xo = x_ref[...] + jnp.dot(merged, wo_ref[...], preferred_element_type=F32)
    xo_ref[...] = xo
    hn_ref[...] = _rms(xo, g_ref[...]).astype(hn_ref.dtype)


def _mix_out(oa, ob, gates, x, w_pa, w_pb, w_out, norm_ffn, layer):
    t = x.shape[0]
    row = lambda width, col=0: pl.BlockSpec((MIX_TM, width), lambda i: (i, col))
    whole = lambda k, n: pl.BlockSpec((None, k, n), lambda i: (layer, 0, 0),
                                      pipeline_mode=pl.Buffered(1))
    return pl.pallas_call(
        _mix_kernel,
        grid=(t // MIX_TM,),
        in_specs=[row(A_WIDTH), row(B_OUT), row(D_MODEL, 0), row(D_MODEL, 1), row(D_MODEL),
                  whole(A_WIDTH, D_MODEL), whole(B_OUT, D_MODEL), whole(D_MODEL, D_MODEL),
                  pl.BlockSpec((None, 1, D_MODEL), lambda i: (layer, 0, 0))],
        out_specs=[row(D_MODEL), row(D_MODEL)],
        out_shape=[jax.ShapeDtypeStruct((t, D_MODEL), F32),
                   jax.ShapeDtypeStruct((t, D_MODEL), BF16)],
        compiler_params=_params("parallel"),
        name="mix_out",
    )(oa, ob, gates, gates, x, w_pa, w_pb, w_out, norm_ffn)


def _ffn_kernel(hn_ref, wg_ref, wu_ref, w2_ref, x_ref, g_ref, *outs):
    acc_ref = outs[0]
    k = pl.program_id(1)

    @pl.when(k == 0)
    def _():
        acc_ref[...] = x_ref[...]

    hn = hn_ref[...]
    width = FFN_TF // FFN_CHUNKS
    acts = []
    for c in range(FFN_CHUNKS):
        cols = slice(c * width, (c + 1) * width)
        hg = jnp.dot(hn, wg_ref[:, cols], preferred_element_type=F32)
        hu = jnp.dot(hn, wu_ref[:, cols], preferred_element_type=F32)
        acts.append((hg * _sigmoid(hg) * hu).astype(BF16))
    act = jnp.concatenate(acts, axis=-1)
    acc_ref[...] += jnp.dot(act, w2_ref[...], preferred_element_type=F32)

    @pl.when(k == pl.num_programs(1) - 1)
    def _():
        outs[-1][...] = _rms(acc_ref[...], g_ref[...]).astype(outs[-1].dtype)


def _ffn(hn, w_ffn_in, w_ffn_out, x, norm_all, layer, norm_layer, emit_x, tm, tile0=0, n_tiles=None):
    if n_tiles is None:
        n_tiles = x.shape[0] // tm
    t = n_tiles * tm
    nk = D_FF // FFN_TF
    row_in = pl.BlockSpec((tm, D_MODEL), lambda i, k: (tile0 + i, 0))
    row_out = pl.BlockSpec((tm, D_MODEL), lambda i, k: (i, 0))
    if norm_all.ndim == 3:
        g_spec = pl.BlockSpec((None, 1, D_MODEL), lambda i, k: (norm_layer, 0, 0))
    else:
        g_spec = pl.BlockSpec((1, D_MODEL), lambda i, k: (0, 0))
    out_specs = [row_out]
    out_shape = [jax.ShapeDtypeStruct((t, D_MODEL), F32)]
    if emit_x:
        out_specs = [row_out, row_out]
        out_shape = out_shape + [jax.ShapeDtypeStruct((t, D_MODEL), BF16)]
    return pl.pallas_call(
        _ffn_kernel,
        grid=(n_tiles, nk),
        in_specs=[row_in,
                  pl.BlockSpec((None, D_MODEL, FFN_TF), lambda i, k: (layer, 0, k)),
                  pl.BlockSpec((None, D_MODEL, FFN_TF), lambda i, k: (layer, 0, nk + k)),
                  pl.BlockSpec((None, FFN_TF, D_MODEL), lambda i, k: (layer, k, 0)),
                  row_in, g_spec],
        out_specs=out_specs,
        out_shape=out_shape,
        compiler_params=_params("parallel", "arbitrary"),
        name="ffn",
    )(hn, w_ffn_in, w_ffn_in, w_ffn_out, x, norm_all)


def _trunk(x_parts, seq, norm_mix, norm_ffn, w_in, gate_bias, diff_lambda, diff_subln,
           w_pa, w_pb, w_out, w_ffn_in, w_ffn_out, norm_final):
    tables = _split_rope_tables(seq, A_HEAD_DIM) + _split_rope_tables(seq, B_HEAD_DIM)
    x, xn = _rmsnorm(x_parts, norm_mix, 0)
    for layer in range(DEPTH):
        a_qk, v_plain, groups = _qkv_proj(xn, w_in, layer, tables, seq)
        gates = _gate_proj(xn, w_in, gate_bias, layer, seq)
        oa = _diff_attention(a_qk, v_plain, diff_lambda, diff_subln, layer, seq)
        ob = _dil_attention(v_plain, groups, seq)
        x, hn = _mix_out(oa, ob, gates, x, w_pa, w_pb, w_out, norm_ffn, layer)
        if layer + 1 < DEPTH:
            tm = FFN_TM_WIDE if x.shape[0] % FFN_TM_WIDE == 0 else FFN_TM
            x, xn = _ffn(hn, w_ffn_in, w_ffn_out, x, norm_mix, layer, layer + 1, True, tm)
    tm = FFN_TM_LAST if all(p.shape[0] % FFN_TM_LAST == 0 for p in x_parts) else FFN_TM
    outs, tile0 = [], 0
    for part in x_parts:
        n_tiles = part.shape[0] // tm
        (y,) = _ffn(hn, w_ffn_in, w_ffn_out, x, norm_final, DEPTH - 1, 0, False, tm, tile0, n_tiles)
        outs.append(y)
        tile0 += n_tiles
    return outs


def kernel(x_prompt, x_sample, norm_mix, norm_ffn, w_in, gate_bias, diff_lambda, diff_subln,
           w_proj_a, w_proj_b, w_out, w_ffn_in, w_ffn_out, norm_final):
    bp, seq, d = x_prompt.shape
    bs = x_sample.shape[0]
    assert x_sample.shape[1:] == (seq, d) and d == D_MODEL
    assert seq % (B_PATTERNS[-1][1] * DIL_TQ) == 0 and seq >= DIL_WIN
    y_prompt, y_sample = _trunk(
        [x_prompt.reshape(bp * seq, d), x_sample.reshape(bs * seq, d)], seq,
        norm_mix.reshape(DEPTH, 1, D_MODEL), norm_ffn.reshape(DEPTH, 1, D_MODEL),
        _prepare_w_in(w_in), gate_bias.reshape(DEPTH, 1, GATE_WIDTH),
        diff_lambda, diff_subln.reshape(DEPTH, 2 * A_HEAD_DIM, 1),
        w_proj_a.astype(BF16), w_proj_b.astype(BF16), w_out.astype(BF16),
        w_ffn_in.astype(BF16), w_ffn_out.astype(BF16), norm_final.reshape(1, D_MODEL))
    return (y_prompt.reshape(bp, seq, d), y_sample.reshape(bs, seq, d))
```

```python
import functools
import math

import jax
import jax.numpy as jnp
from jax import lax
from jax.experimental import pallas as pl
from jax.experimental.pallas import tpu as pltpu

D_MODEL = 2048
DEPTH = 4
A_HEADS = 8
A_HEAD_DIM = 64
A_WIDTH = A_HEADS * 2 * A_HEAD_DIM
B_PATTERNS = ((128, 1), (512, 4), (2048, 16))
B_GROUPS = len(B_PATTERNS)
B_HEADS = 4
B_HEAD_DIM = 128
B_WIDTH = B_GROUPS * B_HEADS * B_HEAD_DIM
B_OUT = B_HEADS * B_HEAD_DIM
QKV_WIDTH = 3 * A_WIDTH + 3 * B_WIDTH
GATE_WIDTH = 2 * D_MODEL
D_FF = 5632
ROPE_THETA = 500000.0
ROPE_FRAC = 4
NORM_EPS = 1e-6
MASK_VALUE = -1e30
LOG2_E = 1.4426950408889634

LANES = 128
BF16_ROWS = 16
MXU_WIDTH = 256
VMEM_LIMIT = 60 * 1024 * 1024

PROJ_TN = 512
DIFF_TQ = 512
DIFF_HEADS = 2
DIL_UNROLL = 16
DIL_TQ = 128
DIL_HALF = 64
DIL_WIN = 256
MAX_ROW_STRIDE = 4
MIX_TM = 512
MIX_TN = 512
FFN_TM = 512
FFN_TM_WIDE = 768
FFN_TM_LAST = 1024
FFN_TF = 512
FFN_CHUNKS = 2
NORM_TM = 512

BF16 = jnp.bfloat16
F32 = jnp.float32


def _params(*sem):
    return pltpu.CompilerParams(dimension_semantics=sem, vmem_limit_bytes=VMEM_LIMIT)


def _sigmoid(x):
    return 0.5 * jnp.tanh(0.5 * x) + 0.5


def _rms(x, g):
    ms = jnp.mean(x * x, axis=-1, keepdims=True)
    return x * lax.rsqrt(ms + NORM_EPS) * g


def _stream_specs(parts, tm):
    specs, firsts, start = [], [], 0
    for arr in parts:
        n = arr.shape[0] // tm
        specs.append(pl.BlockSpec((tm, D_MODEL),
                                  lambda i, *_, start=start, n=n: (jnp.clip(i - start, 0, n - 1), 0)))
        firsts.append(start)
        start += n
    return specs, tuple(firsts)


def _stream_tile(refs, firsts, i):
    x = refs[0][...]
    for ref, first in zip(refs[1:], firsts[1:]):
        x = jnp.where(i >= first, ref[...], x)
    return x


def _rmsnorm_kernel(*refs, firsts):
    x_refs, (g_ref, x_ref, o_ref) = refs[:-3], refs[-3:]
    x = _stream_tile(x_refs, firsts, pl.program_id(0))
    x_ref[...] = x
    o_ref[...] = _rms(x, g_ref[...]).astype(o_ref.dtype)


def _rmsnorm(parts, g_all, layer):
    t = sum(p.shape[0] for p in parts)
    specs, firsts = _stream_specs(parts, NORM_TM)
    row = pl.BlockSpec((NORM_TM, D_MODEL), lambda i: (i, 0))
    return pl.pallas_call(
        functools.partial(_rmsnorm_kernel, firsts=firsts),
        grid=(t // NORM_TM,),
        in_specs=specs + [pl.BlockSpec((None, 1, D_MODEL), lambda i: (layer, 0, 0))],
        out_specs=[row, row],
        out_shape=[jax.ShapeDtypeStruct((t, D_MODEL), F32), jax.ShapeDtypeStruct((t, D_MODEL), BF16)],
        compiler_params=_params("parallel"),
        name="rmsnorm_in",
    )(*parts, g_all)


def _stride_stages(d):
    if d <= MAX_ROW_STRIDE:
        return d, 1
    assert d % MAX_ROW_STRIDE == 0 and d // MAX_ROW_STRIDE <= MAX_ROW_STRIDE
    return MAX_ROW_STRIDE, d // MAX_ROW_STRIDE


def _block_class(blk, d):
    s1, s2 = _stride_stages(d)
    return blk if s2 == 1 else s1 * (blk % s2) + blk // s2


def _proj_kernel(x_ref, w_ref, *rest, seq, rope, d, tn, scaled_tiles):
    if rope:
        c_ref, s_ref, o_ref, *scr = rest
    else:
        o_ref, *scr = rest
    j = pl.program_id(1)
    n_sub = seq // d
    x = x_ref[...]
    assert MXU_WIDTH == 2 * LANES
    for m in range(tn // MXU_WIDTH):
        acc = jnp.dot(x, w_ref[:, m * MXU_WIDTH:(m + 1) * MXU_WIDTH], preferred_element_type=F32)
        halves = [acc[:, :LANES], acc[:, LANES:]]
        if rope:
            halves = [h * c_ref[...] + pltpu.roll(h, LANES // 2, 1) * s_ref[...] for h in halves]
        for c, val in enumerate(halves):
            chunk = m * (MXU_WIDTH // LANES) + c
            sl = slice(chunk * LANES, (chunk + 1) * LANES)
            if scaled_tiles:
                val = val * jnp.where(j < scaled_tiles, A_HEAD_DIM ** -0.5, 1.0).astype(F32)
            if d == 1:
                o_ref[:, sl] = val.astype(o_ref.dtype)
                continue
            buf = scr[chunk]
            buf[...] = val
            s1, s2 = _stride_stages(d)
            if s2 > 1:
                tmp, n1 = scr[tn // LANES + chunk], seq // s1
                for p1 in range(s1):
                    tmp[p1 * n1:(p1 + 1) * n1, :] = buf[pl.ds(p1, n1, stride=s1), :]
                runs = [(tmp, p1 * n1, s2) for p1 in range(s1)]
            else:
                runs = [(buf, 0, s1)]
            blk = 0
            for src, base, stride in runs:
                for p in range(stride):
                    o_ref[blk * n_sub:(blk + 1) * n_sub, sl] = (
                        src[pl.ds(base + p, n_sub, stride=stride), :].astype(o_ref.dtype))
                    blk += 1


def _proj_class(xn, w_in, layer, seq, name, tiles, tn, rope=None, tables=None, d=1, scaled_tiles=0):
    t = xn.shape[0]

    def w_map(i, j):
        tile = tiles[0]
        for idx in range(1, len(tiles)):
            tile = jnp.where(j >= idx, tiles[idx], tile)
        return (layer, 0, tile)

    in_specs = [pl.BlockSpec((seq, D_MODEL), lambda i, j: (i, 0)),
                pl.BlockSpec((None, D_MODEL, tn), w_map)]
    args = [xn, w_in]
    if rope:
        in_specs += [pl.BlockSpec((seq, LANES), lambda i, j: (0, 0))] * 2
        args += list(tables)
    return pl.pallas_call(
        functools.partial(_proj_kernel, seq=seq, rope=rope, d=d, tn=tn, scaled_tiles=scaled_tiles),
        grid=(t // seq, len(tiles)),
        in_specs=in_specs,
        out_specs=pl.BlockSpec((seq, tn), lambda i, j: (i, j)),
        out_shape=jax.ShapeDtypeStruct((t, len(tiles) * tn), BF16),
        scratch_shapes=([pltpu.VMEM((seq, LANES), F32)]
                        * (tn // LANES * (0 if d == 1 else 1 if _stride_stages(d)[1] == 1 else 2))),
        compiler_params=_params("parallel", "arbitrary"),
        name=name,
    )(*args)


def _qkv_proj(xn, w_in, layer, tables, seq):
    tab_a, tab_b = tables[:2], tables[2:]
    wide = 2 * PROJ_TN
    n_a = 3 * A_WIDTH // PROJ_TN
    n_b = B_WIDTH // PROJ_TN
    assert n_b == B_GROUPS and A_WIDTH == wide
    proj = functools.partial(_proj_class, xn, w_in, layer, seq)
    a_qk = proj("proj_a_qk", (0, 1), wide, rope=True, tables=tab_a, scaled_tiles=1)
    v_plain = proj("proj_v_plain", (n_a - 2, n_a - 1, n_a + 2 * n_b), PROJ_TN)
    groups = []
    for gi, (_, d) in enumerate(B_PATTERNS):
        qk = proj(f"proj_b_qk_d{d}", (n_a + gi, n_a + n_b + gi), PROJ_TN,
                  rope=True, tables=tab_b, d=d)
        v = None if d == 1 else proj(f"proj_b_v_d{d}", (n_a + 2 * n_b + gi,), PROJ_TN, d=d)
        groups.append((qk, v))
    return a_qk, v_plain, groups


def _split_order(head_dim):
    half = head_dim // ROPE_FRAC // 2
    x1, x2, rest = [], [], []
    for base in range(0, LANES, head_dim):
        x1 += range(base, base + half)
        x2 += range(base + half, base + 2 * half)
        rest += range(base + 2 * half, base + head_dim)
    keep = LANES // 2 - len(x1)
    return x1 + rest[:keep] + x2 + rest[keep:]


def _lanes_of_first_head(lane, head_dim):
    order = _split_order(head_dim)
    mask, start = None, None
    for pos in range(LANES + 1):
        inside = pos < LANES and order[pos] < head_dim
        if inside and start is None:
            start = pos
        elif not inside and start is not None:
            run = (lane >= start) & (lane < pos)
            mask, start = run if mask is None else mask | run, None
    return mask


def _split_rope_tables(seq, head_dim):
    rot = head_dim // ROPE_FRAC
    heads = LANES // head_dim
    inv = ROPE_THETA ** (-(jnp.arange(0, rot, 2, dtype=F32) / rot))
    ang = jnp.arange(seq, dtype=F32)[:, None] * inv[None, :]
    cos, sin = jnp.tile(jnp.cos(ang), (1, heads)), jnp.tile(jnp.sin(ang), (1, heads))
    rest = LANES // 2 - cos.shape[1]
    one, zero = jnp.ones((seq, rest), F32), jnp.zeros((seq, rest), F32)
    return (jnp.concatenate([cos, one, cos, one], axis=-1),
            jnp.concatenate([-sin, zero, sin, zero], axis=-1))


def _w_in_kernel(w_ref, pa_ref, pb_ref, o_ref):
    j = pl.program_id(1)
    n_qa = 2 * A_WIDTH // PROJ_TN
    n_a = 3 * A_WIDTH // PROJ_TN
    n_b = B_WIDTH // PROJ_TN
    is_a = j < n_qa
    is_b = (j >= n_a) & (j < n_a + 2 * n_b)
    w = w_ref[...].astype(BF16)

    def reorder(p_ref):
        for m in range(PROJ_TN // MXU_WIDTH):
            cols = slice(m * MXU_WIDTH, (m + 1) * MXU_WIDTH)
            o_ref[:, cols] = jnp.dot(w[:, cols], p_ref[...],
                                     preferred_element_type=F32).astype(o_ref.dtype)

    pl.when(is_a)(lambda: reorder(pa_ref))
    pl.when(is_b)(lambda: reorder(pb_ref))

    @pl.when(jnp.logical_not(is_a | is_b))
    def _():
        o_ref[...] = w


def _reorder_matrix(head_dim):
    order = _split_order(head_dim)
    m = [[0.0] * MXU_WIDTH for _ in range(MXU_WIDTH)]
    for base in range(0, MXU_WIDTH, LANES):
        for new, old in enumerate(order):
            m[base + old][base + new] = 1.0
    return jnp.asarray(m, BF16)


def _prepare_w_in(w_in):
    depth, k, n = w_in.shape
    p_spec = pl.BlockSpec((MXU_WIDTH, MXU_WIDTH), lambda l, j: (0, 0))
    return pl.pallas_call(
        _w_in_kernel,
        grid=(depth, n // PROJ_TN),
        in_specs=[pl.BlockSpec((None, k, PROJ_TN), lambda l, j: (l, 0, j)), p_spec, p_spec],
        out_specs=pl.BlockSpec((None, k, PROJ_TN), lambda l, j: (l, 0, j)),
        out_shape=jax.ShapeDtypeStruct(w_in.shape, BF16),
        compiler_params=_params("parallel", "parallel"),
        name="w_in_prepare",
    )(w_in, _reorder_matrix(A_HEAD_DIM), _reorder_matrix(B_HEAD_DIM))


def _nt_dot(a, b):
    return lax.dot_general(a, b, (((1,), (1,)), ((), ())), preferred_element_type=F32)


def _diff_kernel(q_ref, k_ref, v_ref, lam_ref, g_ref, o_ref, *vt_refs, seq, lam_init):
    lf = lam_ref[...]
    lam = (jnp.exp(jnp.sum(lf[0:1] * lf[1:2], axis=-1, keepdims=True))
           - jnp.exp(jnp.sum(lf[2:3] * lf[3:4], axis=-1, keepdims=True)) + lam_init)
    hd = 2 * A_HEAD_DIM
    head_cols = [slice(a * LANES, (a + 1) * LANES) for a in range(len(vt_refs))]
    for vt_ref, cols in zip(vt_refs, head_cols):
        vt_ref[0:hd, :] = v_ref[:, cols].astype(F32).T.astype(BF16)
        vt_ref[hd:, :] = jnp.ones((vt_ref.shape[0] - hd, seq), BF16)
    g = g_ref[...] * (1.0 - lam_init)
    lane = lax.broadcasted_iota(jnp.int32, (DIFF_TQ, LANES), 1)
    first = _lanes_of_first_head(lane, A_HEAD_DIM)

    def scores(a, t):
        q = q_ref[t * DIFF_TQ:(t + 1) * DIFF_TQ, head_cols[a]]
        k = k_ref[:, head_cols[a]]
        zero = jnp.zeros_like(q)
        return (_nt_dot(k, jnp.where(first, q, zero)),
                _nt_dot(k, jnp.where(first, zero, q)))

    items = [(a, t) for a in range(len(vt_refs)) for t in range(seq // DIFF_TQ)]
    s_next = scores(*items[0])
    for i, (a, t) in enumerate(items):
        s1, s2 = s_next
        if i + 1 < len(items):
            s_next = scores(*items[i + 1])
        e1 = jnp.exp(s1 - jnp.max(s1, axis=0, keepdims=True)).astype(BF16)
        e2 = jnp.exp(s2 - jnp.max(s2, axis=0, keepdims=True)).astype(BF16)
        vt = vt_refs[a][...]
        u1 = jnp.dot(vt, e1, preferred_element_type=F32)
        u2 = jnp.dot(vt, e2, preferred_element_type=F32)
        o = u1[:hd] * (1.0 / u1[hd:hd + 1]) - u2[:hd] * (lam / u2[hd:hd + 1])
        ms = jnp.mean(o * o, axis=0, keepdims=True)
        o_ref[t * DIFF_TQ:(t + 1) * DIFF_TQ, head_cols[a]] = (
            (o * lax.rsqrt(ms + NORM_EPS) * g).T.astype(o_ref.dtype))


def _diff_attention(a_qk, v_plain, lam_all, subln_all, layer, seq):
    t = a_qk.shape[0]
    hd = 2 * A_HEAD_DIM
    width = DIFF_HEADS * LANES
    k_off = A_WIDTH // width
    blk = lambda off: pl.BlockSpec((seq, width), lambda b, h: (b, off + h))
    return pl.pallas_call(
        functools.partial(_diff_kernel, seq=seq, lam_init=0.8 - 0.6 * math.exp(-0.3 * layer)),
        grid=(t // seq, A_HEADS // DIFF_HEADS),
        in_specs=[blk(0), blk(k_off), blk(0),
                  pl.BlockSpec((None, 4, A_HEAD_DIM), lambda b, h: (layer, 0, 0)),
                  pl.BlockSpec((None, hd, 1), lambda b, h: (layer, 0, 0))],
        out_specs=blk(0),
        out_shape=jax.ShapeDtypeStruct((t, A_WIDTH), BF16),
        scratch_shapes=[pltpu.VMEM((hd + BF16_ROWS, seq), BF16)] * DIFF_HEADS,
        compiler_params=_params("parallel", "parallel"),
        name="diff_attention",
    )(a_qk, a_qk, v_plain, lam_all, subln_all)


def _dil_kernel(*refs, seq):
    q_refs, k_refs, v_refs = refs[0:3], refs[3:6], refs[6:9]
    o_ref = refs[9]
    o_scr, l_scr = refs[10:13], refs[13:16]
    scale = B_HEAD_DIM ** -0.5
    row = lax.broadcasted_iota(jnp.int32, (DIL_TQ, DIL_WIN), 0)
    col = lax.broadcasted_iota(jnp.int32, (DIL_TQ, DIL_WIN), 1)
    rel = col - row
    col1 = lax.broadcasted_iota(jnp.int32, (1, DIL_WIN), 1)

    def step(t, carry):
        q0 = pl.multiple_of(t * DIL_TQ, DIL_TQ)
        ws = pl.multiple_of(jnp.clip(q0 - DIL_HALF, 0, seq - DIL_WIN), DIL_HALF)
        band = jnp.abs(rel + (ws - q0)) <= DIL_HALF
        for gi, (window, d) in enumerate(B_PATTERNS):
            assert window // (2 * d) == DIL_HALF
            n_sub = seq // d
            nblk = n_sub // DIL_TQ
            run = t // nblk
            n = t % nblk
            lo = run * n_sub - ws
            p = _block_class(run, d)
            q = q_refs[gi][pl.ds(q0, DIL_TQ), :]
            k = k_refs[gi][pl.ds(ws, DIL_WIN), :]
            v = v_refs[gi][pl.ds(ws, DIL_WIN), :]
            s = jnp.where(band, _nt_dot(q, k), MASK_VALUE)
            if d > 1:
                s = s + jnp.where((col1 >= lo) & (col1 < lo + n_sub), 0.0, MASK_VALUE)
            m = jnp.max(s, axis=-1, keepdims=True)
            e = jnp.exp2((s - m) * (scale * LOG2_E))
            l = jnp.sum(e, axis=-1, keepdims=True)
            o = jnp.dot(e.astype(v.dtype), v, preferred_element_type=F32) / l
            lse = jnp.broadcast_to(m * scale + jnp.log(l), (DIL_TQ, LANES))
            dst = pl.ds(n * (DIL_TQ * d) + p, DIL_TQ, stride=d) if d > 1 else pl.ds(q0, DIL_TQ)
            o_scr[gi][dst, :] = o
            l_scr[gi][dst, :] = lse
        return carry

    lax.fori_loop(0, seq // DIL_TQ, step, 0, unroll=DIL_UNROLL)

    l0, l1, l2 = l_scr[0][...], l_scr[1][...], l_scr[2][...]
    m = jnp.maximum(jnp.maximum(l0, l1), l2)
    e0, e1, e2 = jnp.exp(l0 - m), jnp.exp(l1 - m), jnp.exp(l2 - m)
    num = e0 * o_scr[0][...] + e1 * o_scr[1][...] + e2 * o_scr[2][...]
    o_ref[...] = (num / (e0 + e1 + e2)).astype(o_ref.dtype)


def _dil_attention(v_plain, groups, seq):
    t = v_plain.shape[0]
    blk = lambda off: pl.BlockSpec((seq, LANES), lambda b, h: (b, off + h))
    qs = [qk for qk, _ in groups]
    vs = [v_plain if v is None else v for _, v in groups]
    v_off = [A_WIDTH // LANES if v is None else 0 for _, v in groups]
    in_specs = ([blk(0)] * B_GROUPS + [blk(B_HEADS)] * B_GROUPS + [blk(off) for off in v_off])
    return pl.pallas_call(
        functools.partial(_dil_kernel, seq=seq),
        grid=(t // seq, B_HEADS),
        in_specs=in_specs,
        out_specs=pl.BlockSpec((seq, LANES), lambda b, h: (b, h)),
        out_shape=jax.ShapeDtypeStruct((t, B_OUT), BF16),
        scratch_shapes=[pltpu.VMEM((seq, LANES), F32)] * (2 * B_GROUPS),
        compiler_params=_params("parallel", "parallel"),
        name="dilated_attention",
    )(*qs, *qs, *vs)


def _mix_kernel(xn_ref, oa_ref, ob_ref, x_ref, wga_ref, wgb_ref, ba_ref, bb_ref,
                wpa_ref, wpb_ref, wo_ref, g_ref, xo_ref, hn_ref):
    j = pl.program_id(1)

    @pl.when(j == 0)
    def _():
        xo_ref[...] = x_ref[...]

    xn = xn_ref[...]
    ga = _sigmoid(jnp.dot(xn, wga_ref[...], preferred_element_type=F32) + ba_ref[...])
    ya = jnp.dot(oa_ref[...], wpa_ref[...], preferred_element_type=F32)
    gb = _sigmoid(jnp.dot(xn, wgb_ref[...], preferred_element_type=F32) + bb_ref[...])
    yb = jnp.dot(ob_ref[...], wpb_ref[...], preferred_element_type=F32)
    merged = (ga * ya + gb * yb).astype(BF16)
    xo_ref[...] += jnp.dot(merged, wo_ref[...], preferred_element_type=F32)

    @pl.when(j == pl.num_programs(1) - 1)
    def _():
        hn_ref[...] = _rms(xo_ref[...], g_ref[...]).astype(hn_ref.dtype)


def _mix_out(xn, oa, ob, x, w_in, bias, w_pa, w_pb, w_out, norm_ffn, layer):
    t = x.shape[0]
    nj = D_MODEL // MIX_TN
    gate0 = QKV_WIDTH // MIX_TN
    row = lambda width: pl.BlockSpec((MIX_TM, width), lambda i, j: (i, 0))
    cols = lambda k, off: pl.BlockSpec((None, k, MIX_TN), lambda i, j: (layer, 0, off + j))
    return pl.pallas_call(
        _mix_kernel,
        grid=(t // MIX_TM, nj),
        in_specs=[row(D_MODEL), row(A_WIDTH), row(B_OUT), row(D_MODEL),
                  cols(D_MODEL, gate0), cols(D_MODEL, gate0 + nj), cols(1, 0), cols(1, nj),
                  cols(A_WIDTH, 0), cols(B_OUT, 0),
                  pl.BlockSpec((None, MIX_TN, D_MODEL), lambda i, j: (layer, j, 0)),
                  pl.BlockSpec((None, 1, D_MODEL), lambda i, j: (layer, 0, 0))],
        out_specs=[row(D_MODEL), row(D_MODEL)],
        out_shape=[jax.ShapeDtypeStruct((t, D_MODEL), F32),
                   jax.ShapeDtypeStruct((t, D_MODEL), BF16)],
        compiler_params=_params("parallel", "arbitrary"),
        name="mix_out",
    )(xn, oa, ob, x, w_in, w_in, bias, bias, w_pa, w_pb, w_out, norm_ffn)


def _ffn_kernel(hn_ref, wg_ref, wu_ref, w2_ref, x_ref, g_ref, *outs):
    acc_ref = outs[0]
    k = pl.program_id(1)

    @pl.when(k == 0)
    def _():
        acc_ref[...] = x_ref[...]

    hn = hn_ref[...]
    width = FFN_TF // FFN_CHUNKS
    acts = []
    for c in range(FFN_CHUNKS):
        cols = slice(c * width, (c + 1) * width)
        hg = jnp.dot(hn, wg_ref[:, cols], preferred_element_type=F32)
        hu = jnp.dot(hn, wu_ref[:, cols], preferred_element_type=F32)
        acts.append((hg * _sigmoid(hg) * hu).astype(BF16))
    act = jnp.concatenate(acts, axis=-1)
    acc_ref[...] += jnp.dot(act, w2_ref[...], preferred_element_type=F32)

    @pl.when(k == pl.num_programs(1) - 1)
    def _():
        outs[-1][...] = _rms(acc_ref[...], g_ref[...]).astype(outs[-1].dtype)


def _ffn(hn, w_ffn_in, w_ffn_out, x, norm_all, layer, norm_layer, emit_x, tm, tile0=0, n_tiles=None):
    if n_tiles is None:
        n_tiles = x.shape[0] // tm
    t = n_tiles * tm
    nk = D_FF // FFN_TF
    row_in = pl.BlockSpec((tm, D_MODEL), lambda i, k: (tile0 + i, 0))
    row_out = pl.BlockSpec((tm, D_MODEL), lambda i, k: (i, 0))
    if norm_all.ndim == 3:
        g_spec = pl.BlockSpec((None, 1, D_MODEL), lambda i, k: (norm_layer, 0, 0))
    else:
        g_spec = pl.BlockSpec((1, D_MODEL), lambda i, k: (0, 0))
    out_specs = [row_out]
    out_shape = [jax.ShapeDtypeStruct((t, D_MODEL), F32)]
    if emit_x:
        out_specs = [row_out, row_out]
        out_shape = out_shape + [jax.ShapeDtypeStruct((t, D_MODEL), BF16)]
    return pl.pallas_call(
        _ffn_kernel,
        grid=(n_tiles, nk),
        in_specs=[row_in,
                  pl.BlockSpec((None, D_MODEL, FFN_TF), lambda i, k: (layer, 0, k)),
                  pl.BlockSpec((None, D_MODEL, FFN_TF), lambda i, k: (layer, 0, nk + k)),
                  pl.BlockSpec((None, FFN_TF, D_MODEL), lambda i, k: (layer, k, 0)),
                  row_in, g_spec],
        out_specs=out_specs,
        out_shape=out_shape,
        compiler_params=_params("parallel", "arbitrary"),
        name="ffn",
    )(hn, w_ffn_in, w_ffn_in, w_ffn_out, x, norm_all)


def _trunk(x_parts, seq, norm_mix, norm_ffn, w_in, gate_bias, diff_lambda, diff_subln,
           w_pa, w_pb, w_out, w_ffn_in, w_ffn_out, norm_final):
    tables = _split_rope_tables(seq, A_HEAD_DIM) + _split_rope_tables(seq, B_HEAD_DIM)
    x, xn = _rmsnorm(x_parts, norm_mix, 0)
    for layer in range(DEPTH):
        a_qk, v_plain, groups = _qkv_proj(xn, w_in, layer, tables, seq)
        oa = _diff_attention(a_qk, v_plain, diff_lambda, diff_subln, layer, seq)
        ob = _dil_attention(v_plain, groups, seq)
        x, hn = _mix_out(xn, oa, ob, x, w_in, gate_bias, w_pa, w_pb, w_out, norm_ffn, layer)
        if layer + 1 < DEPTH:
            tm = FFN_TM_WIDE if x.shape[0] % FFN_TM_WIDE == 0 else FFN_TM
            x, xn = _ffn(hn, w_ffn_in, w_ffn_out, x, norm_mix, layer, layer + 1, True, tm)
    tm = FFN_TM_LAST if all(p.shape[0] % FFN_TM_LAST == 0 for p in x_parts) else FFN_TM
    outs, tile0 = [], 0
    for part in x_parts:
        n_tiles = part.shape[0] // tm
        (y,) = _ffn(hn, w_ffn_in, w_ffn_out, x, norm_final, DEPTH - 1, 0, False, tm, tile0, n_tiles)
        outs.append(y)
        tile0 += n_tiles
    return outs


def kernel(x_prompt, x_sample, norm_mix, norm_ffn, w_in, gate_bias, diff_lambda, diff_subln,
           w_proj_a, w_proj_b, w_out, w_ffn_in, w_ffn_out, norm_final):
    bp, seq, d = x_prompt.shape
    bs = x_sample.shape[0]
    assert x_sample.shape[1:] == (seq, d) and d == D_MODEL
    assert seq % (B_PATTERNS[-1][1] * DIL_TQ) == 0 and seq >= DIL_WIN
    y_prompt, y_sample = _trunk(
        [x_prompt.reshape(bp * seq, d), x_sample.reshape(bs * seq, d)], seq,
        norm_mix.reshape(DEPTH, 1, D_MODEL), norm_ffn.reshape(DEPTH, 1, D_MODEL),
        _prepare_w_in(w_in), gate_bias.reshape(DEPTH, 1, GATE_WIDTH),
        diff_lambda, diff_subln.reshape(DEPTH, 2 * A_HEAD_DIM, 1),
        w_proj_a.astype(BF16), w_proj_b.astype(BF16), w_out.astype(BF16),
        w_ffn_in.astype(BF16), w_ffn_out.astype(BF16), norm_final.reshape(1, D_MODEL))
    return (y_prompt.reshape(bp, seq, d), y_sample.reshape(bs, seq, d))
```

```python
import functools
import math

import jax
import jax.numpy as jnp
from jax import lax
from jax.experimental import pallas as pl
from jax.experimental.pallas import tpu as pltpu

D_MODEL = 2048
DEPTH = 4
A_HEADS = 8
A_HEAD_DIM = 64
A_WIDTH = A_HEADS * 2 * A_HEAD_DIM
B_PATTERNS = ((128, 1), (512, 4), (2048, 16))
B_GROUPS = len(B_PATTERNS)
B_HEADS = 4
B_HEAD_DIM = 128
B_WIDTH = B_GROUPS * B_HEADS * B_HEAD_DIM
B_OUT = B_HEADS * B_HEAD_DIM
QKV_WIDTH = 3 * A_WIDTH + 3 * B_WIDTH
GATE_WIDTH = 2 * D_MODEL
D_FF = 5632
ROPE_THETA = 500000.0
ROPE_FRAC = 4
NORM_EPS = 1e-6
MASK_VALUE = -1e30
LOG2_E = 1.4426950408889634

LANES = 128
BF16_ROWS = 16
MXU_WIDTH = 256
VMEM_LIMIT = 60 * 1024 * 1024

PROJ_TN = 512
GATE_TILES = 2
DIFF_TQ = 512
DIFF_HEADS = 2
DIL_UNROLL = 16
DIL_TQ = 128
DIL_HEADS = 2
DIL_HALF = 64
DIL_WIN = 256
MAX_ROW_STRIDE = 4
MIX_TM = 512
FFN_TM = 512
FFN_TM_WIDE = 768
FFN_TM_LAST = 1024
FFN_TF = 512
FFN_CHUNKS = 2
NORM_TM = 512

BF16 = jnp.bfloat16
F32 = jnp.float32


def _params(*sem):
    return pltpu.CompilerParams(dimension_semantics=sem, vmem_limit_bytes=VMEM_LIMIT)


def _sigmoid(x):
    return 0.5 * jnp.tanh(0.5 * x) + 0.5


def _rms(x, g):
    ms = jnp.mean(x * x, axis=-1, keepdims=True)
    return x * lax.rsqrt(ms + NORM_EPS) * g


def _stream_specs(parts, tm):
    specs, firsts, start = [], [], 0
    for arr in parts:
        n = arr.shape[0] // tm
        specs.append(pl.BlockSpec((tm, D_MODEL),
                                  lambda i, *_, start=start, n=n: (jnp.clip(i - start, 0, n - 1), 0)))
        firsts.append(start)
        start += n
    return specs, tuple(firsts)


def _stream_tile(refs, firsts, i):
    x = refs[0][...]
    for ref, first in zip(refs[1:], firsts[1:]):
        x = jnp.where(i >= first, ref[...], x)
    return x


def _rmsnorm_kernel(*refs, firsts):
    x_refs, (g_ref, x_ref, o_ref) = refs[:-3], refs[-3:]
    x = _stream_tile(x_refs, firsts, pl.program_id(0))
    x_ref[...] = x
    o_ref[...] = _rms(x, g_ref[...]).astype(o_ref.dtype)


def _rmsnorm(parts, g_all, layer):
    t = sum(p.shape[0] for p in parts)
    specs, firsts = _stream_specs(parts, NORM_TM)
    row = pl.BlockSpec((NORM_TM, D_MODEL), lambda i: (i, 0))
    return pl.pallas_call(
        functools.partial(_rmsnorm_kernel, firsts=firsts),
        grid=(t // NORM_TM,),
        in_specs=specs + [pl.BlockSpec((None, 1, D_MODEL), lambda i: (layer, 0, 0))],
        out_specs=[row, row],
        out_shape=[jax.ShapeDtypeStruct((t, D_MODEL), F32), jax.ShapeDtypeStruct((t, D_MODEL), BF16)],
        compiler_params=_params("parallel"),
        name="rmsnorm_in",
    )(*parts, g_all)


def _stride_stages(d):
    if d <= MAX_ROW_STRIDE:
        return d, 1
    assert d % MAX_ROW_STRIDE == 0 and d // MAX_ROW_STRIDE <= MAX_ROW_STRIDE
    return MAX_ROW_STRIDE, d // MAX_ROW_STRIDE


def _block_class(blk, d):
    s1, s2 = _stride_stages(d)
    return blk if s2 == 1 else s1 * (blk % s2) + blk // s2


def _proj_kernel(x_ref, w_ref, *rest, seq, rope, d, tn, scaled_tiles):
    if rope:
        c_ref, s_ref, o_ref, *scr = rest
    else:
        o_ref, *scr = rest
    j = pl.program_id(1)
    n_sub = seq // d
    x = x_ref[...]
    assert MXU_WIDTH == 2 * LANES
    for m in range(tn // MXU_WIDTH):
        acc = jnp.dot(x, w_ref[:, m * MXU_WIDTH:(m + 1) * MXU_WIDTH], preferred_element_type=F32)
        halves = [acc[:, :LANES], acc[:, LANES:]]
        if rope:
            halves = [h * c_ref[...] + pltpu.roll(h, LANES // 2, 1) * s_ref[...] for h in halves]
        for c, val in enumerate(halves):
            chunk = m * (MXU_WIDTH // LANES) + c
            sl = slice(chunk * LANES, (chunk + 1) * LANES)
            if scaled_tiles:
                val = val * jnp.where(j < scaled_tiles, A_HEAD_DIM ** -0.5, 1.0).astype(F32)
            if d == 1:
                o_ref[:, sl] = val.astype(o_ref.dtype)
                continue
            buf = scr[chunk]
            buf[...] = val
            s1, s2 = _stride_stages(d)
            if s2 > 1:
                tmp, n1 = scr[tn // LANES + chunk], seq // s1
                for p1 in range(s1):
                    tmp[p1 * n1:(p1 + 1) * n1, :] = buf[pl.ds(p1, n1, stride=s1), :]
                runs = [(tmp, p1 * n1, s2) for p1 in range(s1)]
            else:
                runs = [(buf, 0, s1)]
            blk = 0
            for src, base, stride in runs:
                for p in range(stride):
                    o_ref[blk * n_sub:(blk + 1) * n_sub, sl] = (
                        src[pl.ds(base + p, n_sub, stride=stride), :].astype(o_ref.dtype))
                    blk += 1


def _proj_class(xn, w_in, layer, seq, name, tiles, tn, rope=None, tables=None, d=1, scaled_tiles=0):
    t = xn.shape[0]

    def w_map(i, j):
        tile = tiles[0]
        for idx in range(1, len(tiles)):
            tile = jnp.where(j >= idx, tiles[idx], tile)
        return (layer, 0, tile)

    in_specs = [pl.BlockSpec((seq, D_MODEL), lambda i, j: (i, 0)),
                pl.BlockSpec((None, D_MODEL, tn), w_map)]
    args = [xn, w_in]
    if rope:
        in_specs += [pl.BlockSpec((seq, LANES), lambda i, j: (0, 0))] * 2
        args += list(tables)
    return pl.pallas_call(
        functools.partial(_proj_kernel, seq=seq, rope=rope, d=d, tn=tn, scaled_tiles=scaled_tiles),
        grid=(t // seq, len(tiles)),
        in_specs=in_specs,
        out_specs=pl.BlockSpec((seq, tn), lambda i, j: (i, j)),
        out_shape=jax.ShapeDtypeStruct((t, len(tiles) * tn), BF16),
        scratch_shapes=([pltpu.VMEM((seq, LANES), F32)]
                        * (tn // LANES * (0 if d == 1 else 1 if _stride_stages(d)[1] == 1 else 2))),
        compiler_params=_params("parallel", "arbitrary"),
        name=name,
    )(*args)


def _qkv_proj(xn, w_in, layer, tables, seq):
    tab_a, tab_b = tables[:2], tables[2:]
    wide = 2 * PROJ_TN
    n_a = 3 * A_WIDTH // PROJ_TN
    n_b = B_WIDTH // PROJ_TN
    assert n_b == B_GROUPS and A_WIDTH == wide
    proj = functools.partial(_proj_class, xn, w_in, layer, seq)
    a_qk = proj("proj_a_qk", (0, 1), wide, rope=True, tables=tab_a, scaled_tiles=1)
    v_plain = proj("proj_v_plain", (n_a - 2, n_a - 1, n_a + 2 * n_b), PROJ_TN)
    groups = []
    for gi, (_, d) in enumerate(B_PATTERNS):
        qk = proj(f"proj_b_qk_d{d}", (n_a + gi, n_a + n_b + gi), PROJ_TN,
                  rope=True, tables=tab_b, d=d)
        v = None if d == 1 else proj(f"proj_b_v_d{d}", (n_a + 2 * n_b + gi,), PROJ_TN, d=d)
        groups.append((qk, v))
    return a_qk, v_plain, groups


def _split_order(head_dim):
    half = head_dim // ROPE_FRAC // 2
    x1, x2, rest = [], [], []
    for base in range(0, LANES, head_dim):
        x1 += range(base, base + half)
        x2 += range(base + half, base + 2 * half)
        rest += range(base + 2 * half, base + head_dim)
    keep = LANES // 2 - len(x1)
    return x1 + rest[:keep] + x2 + rest[keep:]


def _lanes_of_first_head(lane, head_dim):
    order = _split_order(head_dim)
    mask, start = None, None
    for pos in range(LANES + 1):
        inside = pos < LANES and order[pos] < head_dim
        if inside and start is None:
            start = pos
        elif not inside and start is not None:
            run = (lane >= start) & (lane < pos)
            mask, start = run if mask is None else mask | run, None
    return mask


def _split_rope_tables(seq, head_dim):
    rot = head_dim // ROPE_FRAC
    heads = LANES // head_dim
    inv = ROPE_THETA ** (-(jnp.arange(0, rot, 2, dtype=F32) / rot))
    ang = jnp.arange(seq, dtype=F32)[:, None] * inv[None, :]
    cos, sin = jnp.tile(jnp.cos(ang), (1, heads)), jnp.tile(jnp.sin(ang), (1, heads))
    rest = LANES // 2 - cos.shape[1]
    one, zero = jnp.ones((seq, rest), F32), jnp.zeros((seq, rest), F32)
    return (jnp.concatenate([cos, one, cos, one], axis=-1),
            jnp.concatenate([-sin, zero, sin, zero], axis=-1))


def _w_in_kernel(w_ref, pa_ref, pb_ref, o_ref):
    j = pl.program_id(1)
    n_qa = 2 * A_WIDTH // PROJ_TN
    n_a = 3 * A_WIDTH // PROJ_TN
    n_b = B_WIDTH // PROJ_TN
    is_a = j < n_qa
    is_b = (j >= n_a) & (j < n_a + 2 * n_b)
    w = w_ref[...].astype(BF16)

    def reorder(p_ref):
        for m in range(PROJ_TN // MXU_WIDTH):
            cols = slice(m * MXU_WIDTH, (m + 1) * MXU_WIDTH)
            o_ref[:, cols] = jnp.dot(w[:, cols], p_ref[...],
                                     preferred_element_type=F32).astype(o_ref.dtype)

    pl.when(is_a)(lambda: reorder(pa_ref))
    pl.when(is_b)(lambda: reorder(pb_ref))

    @pl.when(jnp.logical_not(is_a | is_b))
    def _():
        o_ref[...] = w


def _reorder_matrix(head_dim):
    order = _split_order(head_dim)
    m = [[0.0] * MXU_WIDTH for _ in range(MXU_WIDTH)]
    for base in range(0, MXU_WIDTH, LANES):
        for new, old in enumerate(order):
            m[base + old][base + new] = 1.0
    return jnp.asarray(m, BF16)


def _prepare_w_in(w_in):
    depth, k, n = w_in.shape
    p_spec = pl.BlockSpec((MXU_WIDTH, MXU_WIDTH), lambda l, j: (0, 0))
    return pl.pallas_call(
        _w_in_kernel,
        grid=(depth, n // PROJ_TN),
        in_specs=[pl.BlockSpec((None, k, PROJ_TN), lambda l, j: (l, 0, j)), p_spec, p_spec],
        out_specs=pl.BlockSpec((None, k, PROJ_TN), lambda l, j: (l, 0, j)),
        out_shape=jax.ShapeDtypeStruct(w_in.shape, BF16),
        compiler_params=_params("parallel", "parallel"),
        name="w_in_prepare",
    )(w_in, _reorder_matrix(A_HEAD_DIM), _reorder_matrix(B_HEAD_DIM))


def _gate_kernel(x_ref, *refs):
    w_refs, (b_ref, o_ref) = refs[:-2], refs[-2:]
    x = x_ref[...]
    for n, w_ref in enumerate(w_refs):
        for m in range(PROJ_TN // MXU_WIDTH):
            cols = slice(m * MXU_WIDTH, (m + 1) * MXU_WIDTH)
            out_cols = slice(n * PROJ_TN + cols.start, n * PROJ_TN + cols.stop)
            z = jnp.dot(x, w_ref[:, cols], preferred_element_type=F32) + b_ref[:, out_cols]
            o_ref[:, out_cols] = _sigmoid(z)


def _gate_proj(xn, w_in, bias, layer, seq):
    t = xn.shape[0]
    col0 = QKV_WIDTH // PROJ_TN
    wide = GATE_TILES * PROJ_TN
    w_spec = lambda n: pl.BlockSpec((None, D_MODEL, PROJ_TN),
                                    lambda i, j: (layer, 0, col0 + GATE_TILES * j + n))
    return pl.pallas_call(
        _gate_kernel,
        grid=(t // seq, GATE_WIDTH // wide),
        in_specs=[pl.BlockSpec((seq, D_MODEL), lambda i, j: (i, 0))]
        + [w_spec(n) for n in range(GATE_TILES)]
        + [pl.BlockSpec((None, 1, wide), lambda i, j: (layer, 0, j))],
        out_specs=pl.BlockSpec((seq, wide), lambda i, j: (i, j)),
        out_shape=jax.ShapeDtypeStruct((t, GATE_WIDTH), F32),
        compiler_params=_params("parallel", "arbitrary"),
        name="gate_proj",
    )(xn, *([w_in] * GATE_TILES), bias)


def _nt_dot(a, b):
    return lax.dot_general(a, b, (((1,), (1,)), ((), ())), preferred_element_type=F32)


def _diff_kernel(q_ref, k_ref, v_ref, lam_ref, g_ref, o_ref, *vt_refs, seq, lam_init):
    lf = lam_ref[...]
    lam = (jnp.exp(jnp.sum(lf[0:1] * lf[1:2], axis=-1, keepdims=True))
           - jnp.exp(jnp.sum(lf[2:3] * lf[3:4], axis=-1, keepdims=True)) + lam_init)
    hd = 2 * A_HEAD_DIM
    head_cols = [slice(a * LANES, (a + 1) * LANES) for a in range(len(vt_refs))]
    for vt_ref, cols in zip(vt_refs, head_cols):
        vt_ref[0:hd, :] = v_ref[:, cols].astype(F32).T.astype(BF16)
        vt_ref[hd:, :] = jnp.ones((vt_ref.shape[0] - hd, seq), BF16)
    g = g_ref[...] * (1.0 - lam_init)
    lane = lax.broadcasted_iota(jnp.int32, (DIFF_TQ, LANES), 1)
    first = _lanes_of_first_head(lane, A_HEAD_DIM)

    def scores(a, t):
        q = q_ref[t * DIFF_TQ:(t + 1) * DIFF_TQ, head_cols[a]]
        k = k_ref[:, head_cols[a]]
        zero = jnp.zeros_like(q)
        return (_nt_dot(k, jnp.where(first, q, zero)),
                _nt_dot(k, jnp.where(first, zero, q)))

    items = [(a, t) for a in range(len(vt_refs)) for t in range(seq // DIFF_TQ)]
    s_next = scores(*items[0])
    for i, (a, t) in enumerate(items):
        s1, s2 = s_next
        if i + 1 < len(items):
            s_next = scores(*items[i + 1])
        e1 = jnp.exp(s1 - jnp.max(s1, axis=0, keepdims=True)).astype(BF16)
        e2 = jnp.exp(s2 - jnp.max(s2, axis=0, keepdims=True)).astype(BF16)
        vt = vt_refs[a][...]
        u1 = jnp.dot(vt, e1, preferred_element_type=F32)
        u2 = jnp.dot(vt, e2, preferred_element_type=F32)
        o = u1[:hd] * (1.0 / u1[hd:hd + 1]) - u2[:hd] * (lam / u2[hd:hd + 1])
        ms = jnp.mean(o * o, axis=0, keepdims=True)
        o_ref[t * DIFF_TQ:(t + 1) * DIFF_TQ, head_cols[a]] = (
            (o * lax.rsqrt(ms + NORM_EPS) * g).T.astype(o_ref.dtype))


def _diff_attention(a_qk, v_plain, lam_all, subln_all, layer, seq):
    t = a_qk.shape[0]
    hd = 2 * A_HEAD_DIM
    width = DIFF_HEADS * LANES
    k_off = A_WIDTH // width
    blk = lambda off: pl.BlockSpec((seq, width), lambda b, h: (b, off + h))
    return pl.pallas_call(
        functools.partial(_diff_kernel, seq=seq, lam_init=0.8 - 0.6 * math.exp(-0.3 * layer)),
        grid=(t // seq, A_HEADS // DIFF_HEADS),
        in_specs=[blk(0), blk(k_off), blk(0),
                  pl.BlockSpec((None, 4, A_HEAD_DIM), lambda b, h: (layer, 0, 0)),
                  pl.BlockSpec((None, hd, 1), lambda b, h: (layer, 0, 0))],
        out_specs=blk(0),
        out_shape=jax.ShapeDtypeStruct((t, A_WIDTH), BF16),
        scratch_shapes=[pltpu.VMEM((hd + BF16_ROWS, seq), BF16)] * DIFF_HEADS,
        compiler_params=_params("parallel", "parallel"),
        name="diff_attention",
    )(a_qk, a_qk, v_plain, lam_all, subln_all)


def _dil_kernel(*refs, seq):
    q_refs, k_refs, v_refs = refs[0:3], refs[3:6], refs[6:9]
    o_ref = refs[9]
    scr = refs[10:]
    heads = o_ref.shape[1] // LANES
    scale = B_HEAD_DIM ** -0.5
    row = lax.broadcasted_iota(jnp.int32, (DIL_TQ, DIL_WIN), 0)
    col = lax.broadcasted_iota(jnp.int32, (DIL_TQ, DIL_WIN), 1)
    rel = col - row
    col1 = lax.broadcasted_iota(jnp.int32, (1, DIL_WIN), 1)

    def step(t, carry):
        q0 = pl.multiple_of(t * DIL_TQ, DIL_TQ)
        ws = pl.multiple_of(jnp.clip(q0 - DIL_HALF, 0, seq - DIL_WIN), DIL_HALF)
        band = jnp.abs(rel + (ws - q0)) <= DIL_HALF
        for a, (gi, (window, d)) in [(a, g) for a in range(heads) for g in enumerate(B_PATTERNS)]:
            cols = slice(a * LANES, (a + 1) * LANES)
            o_scr = scr[2 * B_GROUPS * a:2 * B_GROUPS * a + B_GROUPS]
            l_scr = scr[2 * B_GROUPS * a + B_GROUPS:2 * B_GROUPS * (a + 1)]
            assert window // (2 * d) == DIL_HALF
            n_sub = seq // d
            nblk = n_sub // DIL_TQ
            run = t // nblk
            n = t % nblk
            lo = run * n_sub - ws
            p = _block_class(run, d)
            q = q_refs[gi][pl.ds(q0, DIL_TQ), cols]
            k = k_refs[gi][pl.ds(ws, DIL_WIN), cols]
            v = v_refs[gi][pl.ds(ws, DIL_WIN), cols]
            s = jnp.where(band, _nt_dot(q, k), MASK_VALUE)
            if d > 1:
                s = s + jnp.where((col1 >= lo) & (col1 < lo + n_sub), 0.0, MASK_VALUE)
            m = jnp.max(s, axis=-1, keepdims=True)
            e = jnp.exp2((s - m) * (scale * LOG2_E))
            l = jnp.sum(e, axis=-1, keepdims=True)
            o = jnp.dot(e.astype(v.dtype), v, preferred_element_type=F32) / l
            lse = jnp.broadcast_to(m * scale + jnp.log(l), (DIL_TQ, LANES))
            dst = pl.ds(n * (DIL_TQ * d) + p, DIL_TQ, stride=d) if d > 1 else pl.ds(q0, DIL_TQ)
            o_scr[gi][dst, :] = o
            l_scr[gi][dst, :] = lse
        return carry

    lax.fori_loop(0, seq // DIL_TQ, step, 0, unroll=DIL_UNROLL)

    for a in range(heads):
        o_scr = scr[2 * B_GROUPS * a:2 * B_GROUPS * a + B_GROUPS]
        l_scr = scr[2 * B_GROUPS * a + B_GROUPS:2 * B_GROUPS * (a + 1)]
        l0, l1, l2 = l_scr[0][...], l_scr[1][...], l_scr[2][...]
        m = jnp.maximum(jnp.maximum(l0, l1), l2)
        e0, e1, e2 = jnp.exp(l0 - m), jnp.exp(l1 - m), jnp.exp(l2 - m)
        num = e0 * o_scr[0][...] + e1 * o_scr[1][...] + e2 * o_scr[2][...]
        o_ref[:, a * LANES:(a + 1) * LANES] = (num / (e0 + e1 + e2)).astype(o_ref.dtype)


def _dil_attention(v_plain, groups, seq):
    t = v_plain.shape[0]
    width = DIL_HEADS * LANES
    steps = B_HEADS // DIL_HEADS
    blk = lambda off: pl.BlockSpec((seq, width), lambda b, h: (b, off + h))
    qs = [qk for qk, _ in groups]
    vs = [v_plain if v is None else v for _, v in groups]
    v_off = [A_WIDTH // width if v is None else 0 for _, v in groups]
    in_specs = ([blk(0)] * B_GROUPS + [blk(steps)] * B_GROUPS + [blk(off) for off in v_off])
    return pl.pallas_call(
        functools.partial(_dil_kernel, seq=seq),
        grid=(t // seq, steps),
        in_specs=in_specs,
        out_specs=blk(0),
        out_shape=jax.ShapeDtypeStruct((t, B_OUT), BF16),
        scratch_shapes=[pltpu.VMEM((seq, LANES), F32)] * (2 * B_GROUPS * DIL_HEADS),
        compiler_params=_params("parallel", "parallel"),
        name="dilated_attention",
    )(*qs, *qs, *vs)


def _mix_kernel(oa_ref, ob_ref, ga_ref, gb_ref, x_ref, wpa_ref, wpb_ref, wo_ref, g_ref,
                xo_ref, hn_ref):
    ya = jnp.dot(oa_ref[...], wpa_ref[...], preferred_element_type=F32)
    yb = jnp.dot(ob_ref[...], wpb_ref[...], preferred_element_type=F32)
    merged = (ga_ref[...] * ya + gb_ref[...] * yb).astype(BF16)
    xo = x_ref[...] + jnp.dot(merged, wo_ref[...], preferred_element_type=F32)
    xo_ref[...] = xo
    hn_ref[...] = _rms(xo, g_ref[...]).astype(hn_ref.dtype)


def _mix_out(oa, ob, gates, x, w_pa, w_pb, w_out, norm_ffn, layer):
    t = x.shape[0]
    row = lambda width, col=0: pl.BlockSpec((MIX_TM, width), lambda i: (i, col))
    whole = lambda k, n: pl.BlockSpec((None, k, n), lambda i: (layer, 0, 0),
                                      pipeline_mode=pl.Buffered(1))
    return pl.pallas_call(
        _mix_kernel,
        grid=(t // MIX_TM,),
        in_specs=[row(A_WIDTH), row(B_OUT), row(D_MODEL, 0), row(D_MODEL, 1), row(D_MODEL),
                  whole(A_WIDTH, D_MODEL), whole(B_OUT, D_MODEL), whole(D_MODEL, D_MODEL),
                  pl.BlockSpec((None, 1, D_MODEL), lambda i: (layer, 0, 0))],
        out_specs=[row(D_MODEL), row(D_MODEL)],
        out_shape=[jax.ShapeDtypeStruct((t, D_MODEL), F32),
                   jax.ShapeDtypeStruct((t, D_MODEL), BF16)],
        compiler_params=_params("parallel"),
        name="mix_out",
    )(oa, ob, gates, gates, x, w_pa, w_pb, w_out, norm_ffn)


def _ffn_kernel(hn_ref, wg_ref, wu_ref, w2_ref, x_ref, g_ref, *outs):
    acc_ref = outs[0]
    k = pl.program_id(1)

    @pl.when(k == 0)
    def _():
        acc_ref[...] = x_ref[...]

    hn = hn_ref[...]
    width = FFN_TF // FFN_CHUNKS
    acts = []
    for c in range(FFN_CHUNKS):
        cols = slice(c * width, (c + 1) * width)
        hg = jnp.dot(hn, wg_ref[:, cols], preferred_element_type=F32)
        hu = jnp.dot(hn, wu_ref[:, cols], preferred_element_type=F32)
        acts.append((hg * _sigmoid(hg) * hu).astype(BF16))
    act = jnp.concatenate(acts, axis=-1)
    acc_ref[...] += jnp.dot(act, w2_ref[...], preferred_element_type=F32)

    @pl.when(k == pl.num_programs(1) - 1)
    def _():
        outs[-1][...] = _rms(acc_ref[...], g_ref[...]).astype(outs[-1].dtype)


def _ffn(hn, w_ffn_in, w_ffn_out, x, norm_all, layer, norm_layer, emit_x, tm, tile0=0, n_tiles=None):
    if n_tiles is None:
        n_tiles = x.shape[0] // tm
    t = n_tiles * tm
    nk = D_FF // FFN_TF
    row_in = pl.BlockSpec((tm, D_MODEL), lambda i, k: (tile0 + i, 0))
    row_out = pl.BlockSpec((tm, D_MODEL), lambda i, k: (i, 0))
    if norm_all.ndim == 3:
        g_spec = pl.BlockSpec((None, 1, D_MODEL), lambda i, k: (norm_layer, 0, 0))
    else:
        g_spec = pl.BlockSpec((1, D_MODEL), lambda i, k: (0, 0))
    out_specs = [row_out]
    out_shape = [jax.ShapeDtypeStruct((t, D_MODEL), F32)]
    if emit_x:
        out_specs = [row_out, row_out]
        out_shape = out_shape + [jax.ShapeDtypeStruct((t, D_MODEL), BF16)]
    return pl.pallas_call(
        _ffn_kernel,
        grid=(n_tiles, nk),
        in_specs=[row_in,
                  pl.BlockSpec((None, D_MODEL, FFN_TF), lambda i, k: (layer, 0, k)),
                  pl.BlockSpec((None, D_MODEL, FFN_TF), lambda i, k: (layer, 0, nk + k)),
                  pl.BlockSpec((None, FFN_TF, D_MODEL), lambda i, k: (layer, k, 0)),
                  row_in, g_spec],
        out_specs=out_specs,
        out_shape=out_shape,
        compiler_params=_params("parallel", "arbitrary"),
        name="ffn",
    )(hn, w_ffn_in, w_ffn_in, w_ffn_out, x, norm_all)


def _trunk(x_parts, seq, norm_mix, norm_ffn, w_in, gate_bias, diff_lambda, diff_subln,
           w_pa, w_pb, w_out, w_ffn_in, w_ffn_out, norm_final):
    tables = _split_rope_tables(seq, A_HEAD_DIM) + _split_rope_tables(seq, B_HEAD_DIM)
    x, xn = _rmsnorm(x_parts, norm_mix, 0)
    for layer in range(DEPTH):
        a_qk, v_plain, groups = _qkv_proj(xn, w_in, layer, tables, seq)
        gates = _gate_proj(xn, w_in, gate_bias, layer, seq)
        oa = _diff_attention(a_qk, v_plain, diff_lambda, diff_subln, layer, seq)
        ob = _dil_attention(v_plain, groups, seq)
        x, hn = _mix_out(oa, ob, gates, x, w_pa, w_pb, w_out, norm_ffn, layer)
        if layer + 1 < DEPTH:
            tm = FFN_TM_WIDE if x.shape[0] % FFN_TM_WIDE == 0 else FFN_TM
            x, xn = _ffn(hn, w_ffn_in, w_ffn_out, x, norm_mix, layer, layer + 1, True, tm)
    tm = FFN_TM_LAST if all(p.shape[0] % FFN_TM_LAST == 0 for p in x_parts) else FFN_TM
    outs, tile0 = [], 0
    for part in x_parts:
        n_tiles = part.shape[0] // tm
        (y,) = _ffn(hn, w_ffn_in, w_ffn_out, x, norm_final, DEPTH - 1, 0, False, tm, tile0, n_tiles)
        outs.append(y)
        tile0 += n_tiles
    return outs


def kernel(x_prompt, x_sample, norm_mix, norm_ffn, w_in, gate_bias, diff_lambda, diff_subln,
           w_proj_a, w_proj_b, w_out, w_ffn_in, w_ffn_out, norm_final):
    bp, seq, d = x_prompt.shape
    bs = x_sample.shape[0]
    assert x_sample.shape[1:] == (seq, d) and d == D_MODEL
    assert seq % (B_PATTERNS[-1][1] * DIL_TQ) == 0 and seq >= DIL_WIN
    y_prompt, y_sample = _trunk(
        [x_prompt.reshape(bp * seq, d), x_sample.reshape(bs * seq, d)], seq,
        norm_mix.reshape(DEPTH, 1, D_MODEL), norm_ffn.reshape(DEPTH, 1, D_MODEL),
        _prepare_w_in(w_in), gate_bias.reshape(DEPTH, 1, GATE_WIDTH),
        diff_lambda, diff_subln.reshape(DEPTH, 2 * A_HEAD_DIM, 1),
        w_proj_a.astype(BF16), w_proj_b.astype(BF16), w_out.astype(BF16),
        w_ffn_in.astype(BF16), w_ffn_out.astype(BF16), norm_final.reshape(1, D_MODEL))
    return (y_prompt.reshape(bp, seq, d), y_sample.reshape(bs, seq, d))
```
